```python
import math
import jax, jax.numpy as jnp
from jax import lax
import numpy as np

D_MODEL = 1024
BATCH = 4
SEQ = 4096
DEPTH = 4
DEC_BATCH = 32
DEC_SEQ = 8
PAST_LEN = 8192
PAGE_SIZE = 128

ATT_HEADS = 8
HEAD_DIM = 64
ATT_WIDTH = ATT_HEADS * HEAD_DIM
Q_BLOCK = 128
SSM_GROUPS = 16
SSM_GROUP_CH = 16
SSM_WIDTH = SSM_GROUPS * SSM_GROUP_CH
SSM_STATE = 64
GMLP_GROUPS = 4
GMLP_GROUP_CH = 64
GMLP_WIDTH = GMLP_GROUPS * GMLP_GROUP_CH
CHUNK = 128
N_BRANCH = 3
D_FF = 4 * D_MODEL
EPS = 1e-6

Q_OFF = 0
K_OFF = Q_OFF + ATT_WIDTH
V_OFF = K_OFF + ATT_WIDTH
F_OFF = V_OFF + ATT_WIDTH
S_OFF = F_OFF + ATT_HEADS
GU_OFF = S_OFF + SSM_WIDTH
GV_OFF = GU_OFF + GMLP_WIDTH
G_OFF = GV_OFF + GMLP_WIDTH
N_IN = G_OFF + N_BRANCH * D_MODEL

kernel_name = 'hybrid_s5_gmlp_fox_adaln_step'


def _rmsnorm(x, g):
    xf = x.astype(jnp.float32)
    y = xf * lax.rsqrt(jnp.mean(xf * xf, axis=-1, keepdims=True) + EPS)
    return (y * g.astype(jnp.float32)).astype(x.dtype)


def _fox_attention(q, k, v, f_q, f_k, q_pos, k_pos):
    bsz, L, H, E = q.shape
    qb = Q_BLOCK if L % Q_BLOCK == 0 else L
    nb = L // qb
    f_kt = jnp.swapaxes(f_k, 1, 2)

    def block(args):
        q_i, f_i, p_i = args
        s = jnp.einsum('bqhe,bshe->bhqs', q_i, k, preferred_element_type=jnp.float32) * (HEAD_DIM ** -0.5)
        s = s + jnp.swapaxes(f_i, 1, 2)[..., :, None] - f_kt[:, :, None, :]
        s = jnp.where((p_i[:, None] >= k_pos[None, :])[None, None], s, -jnp.inf)
        p = jax.nn.softmax(s, axis=-1).astype(v.dtype)
        return jnp.einsum('bhqs,bshe->bqhe', p, v)

    q_b = jnp.moveaxis(q.reshape(bsz, nb, qb, H, E), 1, 0)
    f_b = jnp.moveaxis(f_q.reshape(bsz, nb, qb, H), 1, 0)
    p_b = q_pos.reshape(nb, qb)
    o = lax.map(block, (q_b, f_b, p_b))
    return jnp.moveaxis(o, 0, 1).reshape(bsz, L, H, E)


def _cplx_affine_combine(e1, e2):
    a1r, a1i, b1r, b1i = e1
    a2r, a2i, b2r, b2i = e2
    return (a1r * a2r - a1i * a2i, a1r * a2i + a1i * a2r,
            a2r * b1r - a2i * b1i + b2r, a2r * b1i + a2i * b1r + b2i)


def _s5(u, a_re, a_im, log_dt, b_re, b_im, c_re, c_im, d, h0):
    bsz, L, _ = u.shape
    f32 = jnp.float32
    uf = u.astype(f32)
    ug = uf.reshape(bsz, L, SSM_GROUPS, SSM_GROUP_CH)
    a_re = a_re.astype(f32)
    a_im = a_im.astype(f32)
    dt = jnp.exp(log_dt.astype(f32))[:, None]
    mag = jnp.exp(a_re * dt)
    ab_re = mag * jnp.cos(a_im * dt)
    ab_im = mag * jnp.sin(a_im * dt)
    den = a_re * a_re + a_im * a_im
    z_re = ((ab_re - 1.0) * a_re + ab_im * a_im) / den
    z_im = (ab_im * a_re - (ab_re - 1.0) * a_im) / den
    b_re = b_re.astype(f32)
    b_im = b_im.astype(f32)
    bb_re = z_re[..., None] * b_re - z_im[..., None] * b_im
    bb_im = z_re[..., None] * b_im + z_im[..., None] * b_re
    x_re = jnp.einsum('gpn,blgn->blgp', bb_re, ug)
    x_im = jnp.einsum('gpn,blgn->blgp', bb_im, ug)
    if h0 is not None:
        h0_re = h0[0].astype(f32)
        h0_im = h0[1].astype(f32)
        x_re = x_re.at[:, 0].add(ab_re * h0_re - ab_im * h0_im)
        x_im = x_im.at[:, 0].add(ab_re * h0_im + ab_im * h0_re)
    shape = x_re.shape
    _, _, h_re, h_im = lax.associative_scan(
        _cplx_affine_combine,
        (jnp.broadcast_to(ab_re, shape), jnp.broadcast_to(ab_im, shape), x_re, x_im), axis=1)
    y = (jnp.einsum('gnp,blgp->blgn', c_re.astype(f32), h_re)
         - jnp.einsum('gnp,blgp->blgn', c_im.astype(f32), h_im))
    y = y.reshape(bsz, L, SSM_WIDTH) + d.astype(f32) * uf
    return y.astype(u.dtype), h_re[:, -1], h_im[:, -1]


def _gmlp_gate(gu, gv, w_s, b_s):
    bsz, L, _ = gv.shape
    cl = min(L, CHUNK)
    nc = -(-L // cl)
    lp = nc * cl
    v = gv if lp == L else jnp.pad(gv, ((0, 0), (0, lp - L), (0, 0)))
    vg = v.reshape(bsz, nc, cl, GMLP_GROUPS, GMLP_GROUP_CH)
    w = jnp.tril(w_s[:, :cl, :cl])
    s = jnp.einsum('gts,bcsgd->bctgd', w, vg) + jnp.transpose(b_s[:, :cl])[None, None, :, :, None]
    s = s.reshape(bsz, lp, GMLP_WIDTH)[:, :L]
    return gu * s


def _trunk(x, c, past, W):
    bsz, L, _ = x.shape
    chunk_start = ((L - 1) // CHUNK) * CHUNK
    ks, vs, lfs, hrs, his, gvs = [], [], [], [], [], []
    for l in range(DEPTH):
        ada = (jax.nn.silu(c) @ W['w_ada'][l] + W['b_ada'][l])[:, None, :]
        sh1, sc1, gt1, sh2, sc2, gt2 = jnp.split(ada, 6, axis=-1)
        h = _rmsnorm(x, W['g_norm1'][l]) * (1.0 + sc1) + sh1
        z = h @ W['w_in'][l]

        q = z[..., Q_OFF:Q_OFF + ATT_WIDTH].reshape(bsz, L, ATT_HEADS, HEAD_DIM)
        k = z[..., K_OFF:K_OFF + ATT_WIDTH].reshape(bsz, L, ATT_HEADS, HEAD_DIM)
        v = z[..., V_OFF:V_OFF + ATT_WIDTH].reshape(bsz, L, ATT_HEADS, HEAD_DIM)
        logf = jax.nn.log_sigmoid((z[..., F_OFF:F_OFF + ATT_HEADS] + W['b_f'][l]).astype(jnp.float32))
        if past is None:
            k_all, v_all, lf_all = k, v, logf
        else:
            pt = past['page_table']
            k_all = jnp.concatenate([past['cache_k'][l][pt].reshape(bsz, -1, ATT_HEADS, HEAD_DIM), k], axis=1)
            v_all = jnp.concatenate([past['cache_v'][l][pt].reshape(bsz, -1, ATT_HEADS, HEAD_DIM), v], axis=1)
            lf_all = jnp.concatenate(
                [past['cache_logf'][l][pt].reshape(bsz, -1, ATT_HEADS).astype(jnp.float32), logf], axis=1)
        n_keys = k_all.shape[1]
        f_all = jnp.cumsum(lf_all, axis=1)
        k_pos = jnp.arange(n_keys)
        q_pos = k_pos[n_keys - L:]
        att = _fox_attention(q, k_all, v_all, f_all[:, n_keys - L:], f_all, q_pos, k_pos)
        y_att = att.reshape(bsz, L, ATT_WIDTH) @ W['w_att_out'][l]

        h0 = None if past is None else (past['state_ssm_re'][l], past['state_ssm_im'][l])
        y_s, h_re, h_im = _s5(z[..., S_OFF:S_OFF + SSM_WIDTH], W['ssm_a_re'][l], W['ssm_a_im'][l],
                              W['ssm_log_dt'][l], W['ssm_b_re'][l], W['ssm_b_im'][l],
                              W['ssm_c_re'][l], W['ssm_c_im'][l], W['ssm_d'][l], h0)
        glu = jax.nn.gelu(y_s) @ W['w_glu'][l]
        y_ssm = glu[..., :D_MODEL] * jax.nn.sigmoid(glu[..., D_MODEL:])

        gu = jax.nn.gelu(z[..., GU_OFF:GU_OFF + GMLP_WIDTH])
        gv = _rmsnorm(jax.nn.gelu(z[..., GV_OFF:GV_OFF + GMLP_WIDTH]), W['g_gv'][l])
        y_gmlp = _gmlp_gate(gu, gv, W['w_s'][l], W['b_s'][l]) @ W['w_gmlp_out'][l]

        gates = jax.nn.sigmoid(z[..., G_OFF:G_OFF + N_BRANCH * D_MODEL]).reshape(bsz, L, N_BRANCH, D_MODEL)
        merged = gates[:, :, 0] * y_ssm + gates[:, :, 1] * y_gmlp + gates[:, :, 2] * y_att
        x = x + gt1 * (merged @ W['w_o'][l])

        h2 = _rmsnorm(x, W['g_norm2'][l]) * (1.0 + sc2) + sh2
        x = x + gt2 * (jnp.square(jax.nn.relu(h2 @ W['w_up'][l])) @ W['w_down'][l])

        ks.append(k)
        vs.append(v)
        lfs.append(logf)
        hrs.append(h_re)
        his.append(h_im)
        gvs.append(gv[:, chunk_start:])
    y = _rmsnorm(x, W['g_final'])
    return y, (jnp.stack(ks), jnp.stack(vs), jnp.stack(lfs), jnp.stack(hrs), jnp.stack(his), jnp.stack(gvs))


def setup_inputs(seed: int = 0) -> dict:
    key = jax.random.key(seed)
    kit = iter(jax.random.split(key, 40))
    f32 = jnp.float32

    def nrm(shape, s):
        return jax.random.normal(next(kit), shape, f32) * s

    n_pages = PAST_LEN // PAGE_SIZE
    n_pool = (DEC_BATCH * n_pages * 5) // 4
    page_table = jax.random.permutation(next(kit), n_pool)[:DEC_BATCH * n_pages].reshape(
        DEC_BATCH, n_pages).astype(jnp.int32)
    n_idx = jnp.arange(SSM_STATE, dtype=f32)
    return {
        'x_prompt': nrm((BATCH, SEQ, D_MODEL), 1.0),
        'x_sample': nrm((DEC_BATCH, DEC_SEQ, D_MODEL), 1.0),
        'c_prompt': nrm((BATCH, D_MODEL), 1.0),
        'c_sample': nrm((DEC_BATCH, D_MODEL), 1.0),
        'cache_k': nrm((DEPTH, n_pool, PAGE_SIZE, ATT_HEADS, HEAD_DIM), 1.0),
        'cache_v': nrm((DEPTH, n_pool, PAGE_SIZE, ATT_HEADS, HEAD_DIM), 1.0),
        'cache_logf': jax.nn.log_sigmoid(3.0 + nrm((DEPTH, n_pool, PAGE_SIZE, ATT_HEADS), 1.0)),
        'state_ssm_re': nrm((DEPTH, DEC_BATCH, SSM_GROUPS, SSM_STATE), 0.5),
        'state_ssm_im': nrm((DEPTH, DEC_BATCH, SSM_GROUPS, SSM_STATE), 0.5),
        'page_table': page_table,
        'w_ada': nrm((DEPTH, D_MODEL, 6 * D_MODEL), 0.5 * D_MODEL ** -0.5),
        'b_ada': nrm((DEPTH, 6 * D_MODEL), 0.02),
        'g_norm1': 1.0 + nrm((DEPTH, D_MODEL), 0.02),
        'w_in': nrm((DEPTH, D_MODEL, N_IN), D_MODEL ** -0.5),
        'b_f': jnp.linspace(1.0, 4.0, ATT_HEADS, dtype=f32)[None, :] + nrm((DEPTH, ATT_HEADS), 0.1),
        'ssm_a_re': -0.5 * jnp.exp(nrm((DEPTH, SSM_GROUPS, SSM_STATE), 0.02)),
        'ssm_a_im': math.pi * n_idx[None, None, :] + nrm((DEPTH, SSM_GROUPS, SSM_STATE), 0.01),
        'ssm_log_dt': jax.random.uniform(next(kit), (DEPTH, SSM_GROUPS), f32,
                                         minval=math.log(1e-3), maxval=math.log(1e-1)),
        'ssm_b_re': nrm((DEPTH, SSM_GROUPS, SSM_STATE, SSM_GROUP_CH), (2 * SSM_GROUP_CH) ** -0.5),
        'ssm_b_im': nrm((DEPTH, SSM_GROUPS, SSM_STATE, SSM_GROUP_CH), (2 * SSM_GROUP_CH) ** -0.5),
        'ssm_c_re': nrm((DEPTH, SSM_GROUPS, SSM_GROUP_CH, SSM_STATE), (2 * SSM_STATE) ** -0.5),
        'ssm_c_im': nrm((DEPTH, SSM_GROUPS, SSM_GROUP_CH, SSM_STATE), (2 * SSM_STATE) ** -0.5),
        'ssm_d': nrm((DEPTH, SSM_WIDTH), 0.5),
        'w_glu': nrm((DEPTH, SSM_WIDTH, 2 * D_MODEL), SSM_WIDTH ** -0.5),
        'g_gv': 1.0 + nrm((DEPTH, GMLP_WIDTH), 0.02),
        'w_s': nrm((DEPTH, GMLP_GROUPS, CHUNK, CHUNK), 0.5 * CHUNK ** -0.5),
        'b_s': 1.0 + nrm((DEPTH, GMLP_GROUPS, CHUNK), 0.1),
        'w_gmlp_out': nrm((DEPTH, GMLP_WIDTH, D_MODEL), GMLP_WIDTH ** -0.5),
        'w_att_out': nrm((DEPTH, ATT_WIDTH, D_MODEL), ATT_WIDTH ** -0.5),
        'w_o': nrm((DEPTH, D_MODEL, D_MODEL), D_MODEL ** -0.5),
        'g_norm2': 1.0 + nrm((DEPTH, D_MODEL), 0.02),
        'w_up': nrm((DEPTH, D_MODEL, D_FF), D_MODEL ** -0.5),
        'w_down': nrm((DEPTH, D_FF, D_MODEL), D_FF ** -0.5),
        'g_final': 1.0 + nrm((D_MODEL,), 0.02),
    }


def reference(x_prompt, x_sample, c_prompt, c_sample, cache_k, cache_v, cache_logf, state_ssm_re,
              state_ssm_im, page_table, w_ada, b_ada, g_norm1, w_in, b_f, ssm_a_re, ssm_a_im, ssm_log_dt,
              ssm_b_re, ssm_b_im, ssm_c_re, ssm_c_im, ssm_d, w_glu, g_gv, w_s, b_s, w_gmlp_out, w_att_out,
              w_o, g_norm2, w_up, w_down, g_final):
    W = dict(w_ada=w_ada, b_ada=b_ada, g_norm1=g_norm1, w_in=w_in, b_f=b_f, ssm_a_re=ssm_a_re,
             ssm_a_im=ssm_a_im, ssm_log_dt=ssm_log_dt, ssm_b_re=ssm_b_re, ssm_b_im=ssm_b_im,
             ssm_c_re=ssm_c_re, ssm_c_im=ssm_c_im, ssm_d=ssm_d, w_glu=w_glu, g_gv=g_gv, w_s=w_s, b_s=b_s,
             w_gmlp_out=w_gmlp_out, w_att_out=w_att_out, w_o=w_o, g_norm2=g_norm2, w_up=w_up,
             w_down=w_down, g_final=g_final)
    y_prompt, (k_p, v_p, lf_p, hr_p, hi_p, gv_p) = _trunk(x_prompt, c_prompt, None, W)
    past = dict(page_table=page_table, cache_k=cache_k, cache_v=cache_v, cache_logf=cache_logf,
                state_ssm_re=state_ssm_re, state_ssm_im=state_ssm_im)
    y_sample, (k_s, v_s, lf_s, hr_s, hi_s, gv_s) = _trunk(x_sample, c_sample, past, W)
    return (y_prompt, y_sample, k_p, v_p, lf_p, hr_p, hi_p, gv_p, k_s, v_s, lf_s, hr_s, hi_s, gv_s)
```

```python
import functools
import math

import jax
import jax.numpy as jnp
from jax import lax
from jax.experimental import pallas as pl
from jax.experimental.pallas import tpu as pltpu

F32 = jnp.float32
BF16 = jnp.bfloat16

D_MODEL = 1024
ATT_HEADS = 8
HEAD_DIM = 64
ATT_WIDTH = ATT_HEADS * HEAD_DIM
SSM_GROUPS = 16
SSM_GROUP_CH = 16
SSM_WIDTH = SSM_GROUPS * SSM_GROUP_CH
SSM_STATE = 64
GMLP_GROUPS = 4
GMLP_GROUP_CH = 64
GMLP_WIDTH = GMLP_GROUPS * GMLP_GROUP_CH
CHUNK = 128
N_BRANCH = 3
D_FF = 4 * D_MODEL
EPS = 1e-6

Q_OFF = 0
K_OFF = Q_OFF + ATT_WIDTH
V_OFF = K_OFF + ATT_WIDTH
F_OFF = V_OFF + ATT_WIDTH
S_OFF = F_OFF + ATT_HEADS
GU_OFF = S_OFF + SSM_WIDTH
GV_OFF = GU_OFF + GMLP_WIDTH
G_OFF = GV_OFF + GMLP_WIDTH

LANE = 128
S5_CHUNK = 16
S5_PAIR = 2
VMEM_LIMIT = 56 * 1024 * 1024


def _cparams(sem):
    return pltpu.CompilerParams(dimension_semantics=sem, vmem_limit_bytes=VMEM_LIMIT)


def _dot(a, b):
    return jnp.dot(a, b, preferred_element_type=F32)


def _dot_nt(a, b):
    return lax.dot_general(a, b, (((1,), (1,)), ((), ())), preferred_element_type=F32)


def _split3(x):
    hi = x.astype(BF16)
    r1 = x - hi.astype(F32)
    mid = r1.astype(BF16)
    lo = (r1 - mid.astype(F32)).astype(BF16)
    return hi, mid, lo


def _gelu(x):
    return 0.5 * x * (1.0 + jnp.tanh(math.sqrt(2.0 / math.pi) * (x + 0.044715 * (x * x * x))))


def _log_sigmoid(x):
    return -(jnp.maximum(-x, 0.0) + jnp.log1p(jnp.exp(-jnp.abs(x))))


def _rms(x):
    return x * lax.rsqrt(jnp.mean(x * x, axis=-1, keepdims=True) + EPS)


def _ada_kernel(c_ref, w_ref, b_ref, o_ref):
    c = c_ref[...]
    a = (c * jax.nn.sigmoid(c)).astype(BF16)
    o_ref[0] = _dot(a, w_ref[0].astype(BF16)) + b_ref[0]


def _ada_call(c_all, w_ada, b_ada):
    depth, d, n = w_ada.shape
    r = c_all.shape[0]
    tn = 1536
    return pl.pallas_call(
        _ada_kernel,
        grid=(depth, n // tn),
        in_specs=[pl.BlockSpec((r, d), lambda l, j: (0, 0)),
                  pl.BlockSpec((1, d, tn), lambda l, j: (l, 0, j)),
                  pl.BlockSpec((1, 1, tn), lambda l, j: (l, 0, j))],
        out_specs=pl.BlockSpec((1, r, tn), lambda l, j: (l, 0, j)),
        out_shape=jax.ShapeDtypeStruct((depth, r, n), F32),
        compiler_params=_cparams(("arbitrary", "arbitrary")),
        name="ada",
    )(c_all, w_ada, b_ada.reshape(depth, 1, n))


def _mod_spec(ada, tiles_per_batch, j):
    r = ada.shape[1]
    return pl.BlockSpec((1, r, D_MODEL), lambda i: (i // tiles_per_batch, 0, j))


W_IN_COLS = 3 * ATT_WIDTH + SSM_WIDTH + 2 * GMLP_WIDTH + LANE


def _in_proj_kernel(x_ref, sh_ref, sc_ref, g1_ref, w_ref, bf_ref, ggv_ref, wmix_ref, bmix_ref, tri_ref,
                    q_ref, kf_ref, kb_ref, vf_ref, vb_ref, lf_ref, cum_ref, u_ref, gated_ref, gv_ref,
                    carry_ref, *, tiles_per_batch, mix_rows):
    i = pl.program_id(0)
    tm = x_ref.shape[0]
    h = (_rms(x_ref[...]) * g1_ref[...] * (1.0 + sc_ref[0]) + sh_ref[0]).astype(BF16)

    a = ATT_WIDTH
    q_ref[...] = (_dot(h, w_ref[:, 0:a]) * (HEAD_DIM ** -0.5)).astype(BF16)
    k = _dot(h, w_ref[:, a:2 * a])
    kf_ref[...] = k
    kb_ref[...] = k.astype(BF16)
    v = _dot(h, w_ref[:, 2 * a:3 * a])
    vf_ref[...] = v
    vb_ref[...] = v.astype(BF16)
    o = 3 * a
    u_ref[...] = _dot(h, w_ref[:, o:o + SSM_WIDTH])
    o += SSM_WIDTH
    gu = _gelu(_dot(h, w_ref[:, o:o + GMLP_WIDTH]))
    o += GMLP_WIDTH
    gv = _rms(_gelu(_dot(h, w_ref[:, o:o + GMLP_WIDTH]))) * ggv_ref[...]
    gv_ref[...] = gv
    o += GMLP_WIDTH

    lf = _log_sigmoid(_dot(h, w_ref[:, o:o + LANE]) + bf_ref[...])
    lf_ref[...] = lf[:, :ATT_HEADS]

    @pl.when(i % tiles_per_batch == 0)
    def _():
        carry_ref[...] = jnp.zeros_like(carry_ref)

    hi, mid, lo = _split3(lf)
    tri = tri_ref[...]
    cum = (_dot(tri, hi) + _dot(tri, mid)) + _dot(tri, lo) + carry_ref[...]
    cum_ref[...] = cum[:, :ATT_HEADS]
    carry_ref[...] = cum[tm - 1:tm, :]

    lane = lax.broadcasted_iota(jnp.int32, (1, GMLP_WIDTH), 1)
    for c in range(tm // mix_rows):
        rows = slice(c * mix_rows, (c + 1) * mix_rows)
        gvc = gv[rows]
        s = bmix_ref[...]
        for g in range(GMLP_GROUPS):
            in_g = (lane >= g * GMLP_GROUP_CH) & (lane < (g + 1) * GMLP_GROUP_CH)
            s = s + _dot(wmix_ref[g], jnp.where(in_g, gvc, 0.0).astype(BF16))
        gated_ref[rows, :] = (gu[rows] * s).astype(BF16)


def _in_proj_call(x, ada, g1, w, bf, ggv, wmix, bmix, tri, *, tm, tiles_per_batch):
    t = x.shape[0]
    mix_rows = wmix.shape[1]
    row = lambda n: pl.BlockSpec((tm, n), lambda i: (i, 0))
    full = lambda arr: pl.BlockSpec(arr.shape, lambda i: (0,) * arr.ndim)
    outs = [(ATT_WIDTH, BF16), (ATT_WIDTH, F32), (ATT_WIDTH, BF16), (ATT_WIDTH, F32), (ATT_WIDTH, BF16),
            (ATT_HEADS, F32), (ATT_HEADS, F32), (SSM_WIDTH, F32), (GMLP_WIDTH, BF16), (GMLP_WIDTH, F32)]
    return pl.pallas_call(
        functools.partial(_in_proj_kernel, tiles_per_batch=tiles_per_batch, mix_rows=mix_rows),
        grid=(t // tm,),
        in_specs=[row(D_MODEL), _mod_spec(ada, tiles_per_batch, 0), _mod_spec(ada, tiles_per_batch, 1),
                  full(g1), full(w), full(bf), full(ggv), full(wmix), full(bmix), full(tri)],
        out_specs=[row(n) for n, _ in outs],
        out_shape=[jax.ShapeDtypeStruct((t, n), dt) for n, dt in outs],
        scratch_shapes=[pltpu.VMEM((1, LANE), F32)],
        compiler_params=_cparams(("arbitrary",)),
        name="in_proj",
    )(x, ada, ada, g1, w, bf, ggv, wmix, bmix, tri)


def _flash_kernel(qi_tab, ki_tab, q_ref, k_ref, v_ref, fq_ref, fk_ref, o_ref, m_sc, l_sc, acc_sc, fq_sc):
    hp = pl.program_id(1)
    s_idx = pl.program_id(2)
    qi = qi_tab[s_idx]
    ki = ki_tab[s_idx]
    tq = q_ref.shape[0]
    tk = k_ref.shape[0]
    lane = lax.broadcasted_iota(jnp.int32, (1, LANE), 1)
    head_lanes = [lane < HEAD_DIM, lane >= HEAD_DIM]

    @pl.when(ki == 0)
    def _():
        m_sc[...] = jnp.full_like(m_sc, -jnp.inf)
        l_sc[...] = jnp.zeros_like(l_sc)
        acc_sc[...] = jnp.zeros_like(acc_sc)
        h8 = lax.broadcasted_iota(jnp.int32, (1, ATT_HEADS), 1)
        for hh in range(2):
            fq_sc[hh] = jnp.sum(jnp.where(h8 == 2 * hp + hh, fq_ref[...], 0.0), axis=-1, keepdims=True)

    def step(masked):
        q = q_ref[...]
        k = k_ref[...]
        v = v_ref[...]
        for hh in range(2):
            s = _dot_nt(jnp.where(head_lanes[hh], q, jnp.zeros_like(q)), k)
            s = s + fq_sc[hh] - fk_ref[0, pl.ds(2 * hp + hh, 1), :]
            if masked:
                r = lax.broadcasted_iota(jnp.int32, (tq, tk), 0)
                c = lax.broadcasted_iota(jnp.int32, (tq, tk), 1)
                s = jnp.where(r >= c, s, -jnp.inf)
            m_prev = m_sc[hh]
            m_new = jnp.maximum(m_prev, jnp.max(s, axis=-1, keepdims=True))
            alpha = jnp.exp(m_prev - m_new)
            p = jnp.exp(s - m_new)
            l_sc[hh] = alpha * l_sc[hh] + jnp.sum(p, axis=-1, keepdims=True)
            acc_sc[hh] = alpha * acc_sc[hh] + _dot(p.astype(BF16), v)
            m_sc[hh] = m_new

    @pl.when(ki < qi)
    def _():
        step(False)

    @pl.when(ki == qi)
    def _():
        step(True)
        o0 = acc_sc[0] / l_sc[0]
        o1 = acc_sc[1] / l_sc[1]
        o_ref[...] = jnp.where(head_lanes[0], o0, o1).astype(o_ref.dtype)


def _flash_call(q, k, v, cum, cum_t, *, bsz, seq, tq):
    nq = seq // tq
    tri = [(a, b) for a in range(nq) for b in range(a + 1)]
    qi_tab = jnp.array([a for a, _ in tri], jnp.int32)
    ki_tab = jnp.array([b for _, b in tri], jnp.int32)
    q_spec = pl.BlockSpec((tq, LANE), lambda b, hp, s, qt, kt: (b * nq + qt[s], hp))
    kv_spec = pl.BlockSpec((tq, LANE), lambda b, hp, s, qt, kt: (b * nq + kt[s], hp))
    return pl.pallas_call(
        _flash_kernel,
        grid_spec=pltpu.PrefetchScalarGridSpec(
            num_scalar_prefetch=2,
            grid=(bsz, ATT_HEADS // 2, len(tri)),
            in_specs=[q_spec, kv_spec, kv_spec,
                      pl.BlockSpec((tq, ATT_HEADS), lambda b, hp, s, qt, kt: (b * nq + qt[s], 0)),
                      pl.BlockSpec((1, ATT_HEADS, tq), lambda b, hp, s, qt, kt: (b, 0, kt[s]))],
            out_specs=q_spec,
            scratch_shapes=[pltpu.VMEM((2, tq, 1), F32), pltpu.VMEM((2, tq, 1), F32),
                            pltpu.VMEM((2, tq, LANE), F32), pltpu.VMEM((2, tq, 1), F32)]),
        out_shape=jax.ShapeDtypeStruct(q.shape, BF16),
        compiler_params=_cparams(("arbitrary", "arbitrary", "arbitrary")),
        name="flash",
    )(qi_tab, ki_tab, q, k, v, cum, cum_t)


DEC_PAGES = 8


def _decode_kernel(pt_ref, q_ref, kn_ref, vn_ref, cn_ref, cnt_ref, hmask_ref, qmask_ref, nmask_ref, tri_ref,
                   ck_hbm, cv_hbm, clf_hbm, o_ref, kbuf, vbuf, lbuf, sem, *, layer, n_pages):
    b = pl.program_id(0)
    nb = pl.num_programs(0)
    page = kbuf.shape[2]
    n_chunks = n_pages // DEC_PAGES
    rows = q_ref.shape[0] * ATT_HEADS

    def copies(bb, chunk, slot):
        out = []
        for j in range(DEC_PAGES):
            pg = pt_ref[bb, chunk * DEC_PAGES + j]
            out.append(pltpu.make_async_copy(ck_hbm.at[layer, pg], kbuf.at[slot, j], sem.at[slot, 0]))
            out.append(pltpu.make_async_copy(cv_hbm.at[layer, pg], vbuf.at[slot, j], sem.at[slot, 1]))
            out.append(pltpu.make_async_copy(clf_hbm.at[layer, pg], lbuf.at[slot, j], sem.at[slot, 2]))
        return out

    def chunk_of(step):
        return n_chunks - 1 - step

    @pl.when(b == 0)
    def _():
        for cp in copies(b, chunk_of(0), 0):
            cp.start()

    q = q_ref[...]
    qbd = jnp.broadcast_to(q[:, None, :], (q.shape[0], ATT_HEADS, ATT_WIDTH)).reshape(rows, ATT_WIDTH)
    qbd = jnp.where(hmask_ref[...] > 0, qbd, jnp.zeros_like(qbd))
    cn = cn_ref[...]
    cn_rows = jnp.broadcast_to(cn[:, None, :], (cn.shape[0], ATT_HEADS, ATT_HEADS)).reshape(rows, ATT_HEADS)
    fq = jnp.sum(jnp.where(qmask_ref[...] > 0, cn_rows, 0.0), axis=-1, keepdims=True)

    s = _dot_nt(qbd, kn_ref[...])
    fk_new = jnp.concatenate([cnt_ref[0]] * q.shape[0], axis=0)
    s = jnp.where(nmask_ref[...] > 0, s + fq - fk_new, -jnp.inf)
    m = jnp.max(s, axis=-1, keepdims=True)
    p = jnp.exp(s - m)
    l = jnp.sum(p, axis=-1, keepdims=True)
    acc = _dot(p.astype(BF16), vn_ref[...])
    run = jnp.zeros((ATT_HEADS, 1), F32)

    for step in range(n_chunks):
        slot = step % 2
        for cp in copies(b, chunk_of(step), slot):
            cp.wait()
        if step + 1 < n_chunks:
            for cp in copies(b, chunk_of(step + 1), 1 - slot):
                cp.start()
        else:
            @pl.when(b + 1 < nb)
            def _():
                for cp in copies(b + 1, chunk_of(0), 1 - slot):
                    cp.start()

        kc = kbuf[slot].reshape(DEC_PAGES * page, ATT_WIDTH).astype(BF16)
        vc = vbuf[slot].reshape(DEC_PAGES * page, ATT_WIDTH).astype(BF16)
        lf = lbuf[slot].reshape(DEC_PAGES * ATT_HEADS, page)
        hi, mid, lo = _split3(lf)
        tri = tri_ref[...]
        suf = (_dot(hi, tri) + _dot(mid, tri)) + _dot(lo, tri)
        tot = jnp.sum(lf, axis=-1, keepdims=True)
        bias = []
        for j in reversed(range(DEC_PAGES)):
            pr = slice(j * ATT_HEADS, (j + 1) * ATT_HEADS)
            bias.append(suf[pr] + run)
            run = run + tot[pr]
        bias = jnp.concatenate(bias[::-1], axis=1)
        s = _dot_nt(qbd, kc) + fq + jnp.concatenate([bias] * q.shape[0], axis=0)
        m_new = jnp.maximum(m, jnp.max(s, axis=-1, keepdims=True))
        alpha = jnp.exp(m - m_new)
        p = jnp.exp(s - m_new)
        l = alpha * l + jnp.sum(p, axis=-1, keepdims=True)
        acc = alpha * acc + _dot(p.astype(BF16), vc)
        m = m_new

    out = jnp.where(hmask_ref[...] > 0, acc / l, 0.0)
    o_ref[...] = jnp.sum(out.reshape(q.shape[0], ATT_HEADS, ATT_WIDTH), axis=1).astype(o_ref.dtype)


def _decode_call(page_table, q, k_new, v_new, cn, cn_t, cache_k, cache_v, cache_lf_t, *, layer, bsz, n_new):
    n_pages = page_table.shape[1]
    page = cache_k.shape[2]
    rows = n_new * ATT_HEADS
    r = jnp.arange(rows)
    hmask = (r[:, None] % ATT_HEADS == jnp.arange(ATT_WIDTH)[None, :] // HEAD_DIM).astype(F32)
    qmask = (r[:, None] % ATT_HEADS == jnp.arange(ATT_HEADS)[None, :]).astype(F32)
    nmask = (r[:, None] // ATT_HEADS >= jnp.arange(n_new)[None, :]).astype(F32)
    tri = (jnp.arange(page)[:, None] > jnp.arange(page)[None, :]).astype(BF16)
    tok = lambda n: pl.BlockSpec((n_new, n), lambda b, pt: (b, 0))
    full = lambda arr: pl.BlockSpec(arr.shape, lambda b, pt: (0,) * arr.ndim)
    hbm = pl.BlockSpec(memory_space=pl.ANY)
    return pl.pallas_call(
        functools.partial(_decode_kernel, layer=layer, n_pages=n_pages),
        grid_spec=pltpu.PrefetchScalarGridSpec(
            num_scalar_prefetch=1,
            grid=(bsz,),
            in_specs=[tok(ATT_WIDTH), tok(ATT_WIDTH), tok(ATT_WIDTH), tok(ATT_HEADS),
                      pl.BlockSpec((1, ATT_HEADS, n_new), lambda b, pt: (b, 0, 0)),
                      full(hmask), full(qmask), full(nmask), full(tri), hbm, hbm, hbm],
            out_specs=tok(ATT_WIDTH),
            scratch_shapes=[pltpu.VMEM((2, DEC_PAGES, page, ATT_WIDTH), F32),
                            pltpu.VMEM((2, DEC_PAGES, page, ATT_WIDTH), F32),
                            pltpu.VMEM((2, DEC_PAGES, ATT_HEADS, page), F32),
                            pltpu.SemaphoreType.DMA((2, 3))]),
        out_shape=jax.ShapeDtypeStruct(q.shape, BF16),
        compiler_params=_cparams(("arbitrary",)),
        name="decode",
    )(page_table, q, k_new, v_new, cn, cn_t, hmask, qmask, nmask, tri, cache_k, cache_v, cache_lf_t)


def _s5_discretise(a_re, a_im, log_dt, b_re, b_im):
    dt = jnp.exp(log_dt)[:, None]
    rate = a_re * dt
    ang = a_im * dt
    mag = jnp.exp(rate)
    ab_re = mag * jnp.cos(ang)
    ab_im = mag * jnp.sin(ang)
    den = a_re * a_re + a_im * a_im
    z_re = ((ab_re - 1.0) * a_re + ab_im * a_im) / den
    z_im = (ab_im * a_re - (ab_re - 1.0) * a_im) / den
    bb_re = z_re[..., None] * b_re - z_im[..., None] * b_im
    bb_im = z_re[..., None] * b_im + z_im[..., None] * b_re
    return rate, ang, ab_re, ab_im, bb_re, bb_im


def _abar_pow(rate, ang, j):
    mag = jnp.exp(rate * j)
    return mag * jnp.cos(ang * j), mag * jnp.sin(ang * j)


def _blockdiag_pairs(m):
    g, r, c = m.shape
    m = m.reshape(g // S5_PAIR, S5_PAIR, r, c)
    eye = jnp.eye(S5_PAIR, dtype=m.dtype)
    return jnp.einsum("pirc,ij->pirjc", m, eye).reshape(g // S5_PAIR, S5_PAIR * r, S5_PAIR * c)


def _s5_chunk_weights(a_re, a_im, log_dt, b_re, b_im, c_re, c_im, chunks_per_seq):
    hp = lax.Precision.HIGHEST
    rate, ang, _, _, bb_re, bb_im = _s5_discretise(a_re, a_im, log_dt, b_re, b_im)
    c = S5_CHUNK
    j = jnp.arange(c + 1, dtype=F32)[:, None, None]
    p_re, p_im = _abar_pow(rate[None], ang[None], j)
    cb_re = jnp.einsum("gmp,jgp->jgmp", c_re, p_re[:c], precision=hp) - jnp.einsum("gmp,jgp->jgmp", c_im, p_im[:c], precision=hp)
    cb_im = jnp.einsum("gmp,jgp->jgmp", c_re, p_im[:c], precision=hp) + jnp.einsum("gmp,jgp->jgmp", c_im, p_re[:c], precision=hp)
    kern = jnp.einsum("jgmp,gpn->jgmn", cb_re, bb_re, precision=hp) - jnp.einsum("jgmp,gpn->jgmn", cb_im, bb_im, precision=hp)
    s_idx = jnp.arange(c)[:, None]
    t_idx = jnp.arange(c)[None, :]
    lag = t_idx - s_idx
    toe = jnp.where((lag >= 0)[:, :, None, None, None], kern[jnp.clip(lag, 0)], 0.0)
    toe = jnp.transpose(toe, (2, 0, 4, 1, 3)).reshape(SSM_GROUPS, c * SSM_GROUP_CH, c * SSM_GROUP_CH)
    q_re, q_im = p_re[:c][::-1], p_im[:c][::-1]
    e_re = jnp.einsum("sgp,gpn->gsnp", q_re, bb_re, precision=hp) - jnp.einsum("sgp,gpn->gsnp", q_im, bb_im, precision=hp)
    e_im = jnp.einsum("sgp,gpn->gsnp", q_re, bb_im, precision=hp) + jnp.einsum("sgp,gpn->gsnp", q_im, bb_re, precision=hp)
    e_re = e_re.reshape(SSM_GROUPS, c * SSM_GROUP_CH, SSM_STATE)
    e_im = e_im.reshape(SSM_GROUPS, c * SSM_GROUP_CH, SSM_STATE)
    f_re = jnp.einsum("gmp,tgp->gptm", c_re, p_re[1:], precision=hp) - jnp.einsum("gmp,tgp->gptm", c_im, p_im[1:], precision=hp)
    f_im = -(jnp.einsum("gmp,tgp->gptm", c_re, p_im[1:], precision=hp) + jnp.einsum("gmp,tgp->gptm", c_im, p_re[1:], precision=hp))
    f_re = f_re.reshape(SSM_GROUPS, SSM_STATE, c * SSM_GROUP_CH)
    f_im = f_im.reshape(SSM_GROUPS, SSM_STATE, c * SSM_GROUP_CH)
    n_steps = max(1, (chunks_per_seq - 1).bit_length())
    d = (c * 2.0 ** jnp.arange(n_steps, dtype=F32))[:, None, None]
    s_re, s_im = _abar_pow(rate[None], ang[None], d)
    to_lanes = lambda a: a.reshape(n_steps, SSM_GROUPS // S5_PAIR, 1, S5_PAIR * SSM_STATE).transpose(1, 0, 2, 3)
    bd = lambda m: _blockdiag_pairs(m).astype(BF16)
    return bd(toe), bd(e_re), bd(e_im), bd(f_re), bd(f_im), to_lanes(s_re), to_lanes(s_im)


def _s5_chunk_kernel(u_ref, toe_ref, ere_ref, eim_ref, fre_ref, fim_ref, sre_ref, sim_ref,
                     y_ref, hre_ref, him_ref, *, chunks_per_seq):
    u = u_ref[0]
    n = u.shape[0]
    h_re = _dot(u, ere_ref[0])
    h_im = _dot(u, eim_ref[0])
    pos = lax.broadcasted_iota(jnp.int32, (n, 1), 0) % chunks_per_seq
    n_steps = sre_ref.shape[1]
    for k in range(n_steps):
        d = 1 << k
        keep = pos >= d
        p_re = jnp.where(keep, pltpu.roll(h_re, d, 0), 0.0)
        p_im = jnp.where(keep, pltpu.roll(h_im, d, 0), 0.0)
        a_re = sre_ref[0, k]
        a_im = sim_ref[0, k]
        h_re, h_im = h_re + (a_re * p_re - a_im * p_im), h_im + (a_re * p_im + a_im * p_re)
    for b in range(n // chunks_per_seq):
        last = (b + 1) * chunks_per_seq - 1
        hre_ref[0, b:b + 1, :] = h_re[last:last + 1]
        him_ref[0, b:b + 1, :] = h_im[last:last + 1]
    keep = pos >= 1
    prev_re = jnp.where(keep, pltpu.roll(h_re, 1, 0), 0.0).astype(BF16)
    prev_im = jnp.where(keep, pltpu.roll(h_im, 1, 0), 0.0).astype(BF16)
    y_ref[0] = (_dot(u, toe_ref[0]) + _dot(prev_re, fre_ref[0])) + _dot(prev_im, fim_ref[0])


def _s5_chunk_call(u, weights, *, bsz, seq):
    toe, e_re, e_im, f_re, f_im, s_re, s_im = weights
    c = S5_CHUNK
    n_pairs = SSM_GROUPS // S5_PAIR
    n_chunks = bsz * seq // c
    width = S5_PAIR * c * SSM_GROUP_CH
    up = u.astype(BF16).reshape(n_chunks, c, n_pairs, S5_PAIR, SSM_GROUP_CH)
    up = jnp.transpose(up, (2, 0, 3, 1, 4)).reshape(n_pairs, n_chunks, width)
    per = lambda arr: pl.BlockSpec((1,) + arr.shape[1:], lambda g: (g,) + (0,) * (arr.ndim - 1))
    y, h_re, h_im = pl.pallas_call(
        functools.partial(_s5_chunk_kernel, chunks_per_seq=seq // c),
        grid=(n_pairs,),
        in_specs=[per(up), per(toe), per(e_re), per(e_im), per(f_re), per(f_im), per(s_re), per(s_im)],
        out_specs=[pl.BlockSpec((1, n_chunks, width), lambda g: (g, 0, 0)),
                   pl.BlockSpec((1, bsz, S5_PAIR * SSM_STATE), lambda g: (g, 0, 0)),
                   pl.BlockSpec((1, bsz, S5_PAIR * SSM_STATE), lambda g: (g, 0, 0))],
        out_shape=[jax.ShapeDtypeStruct((n_pairs, n_chunks, width), F32),
                   jax.ShapeDtypeStruct((n_pairs, bsz, S5_PAIR * SSM_STATE), F32),
                   jax.ShapeDtypeStruct((n_pairs, bsz, S5_PAIR * SSM_STATE), F32)],
        compiler_params=_cparams(("arbitrary",)),
        name="s5_chunk",
    )(up, toe, e_re, e_im, f_re, f_im, s_re, s_im)
    y = y.reshape(n_pairs, n_chunks, S5_PAIR, c, SSM_GROUP_CH)
    y = jnp.transpose(y, (1, 3, 0, 2, 4)).reshape(bsz * seq, SSM_WIDTH)
    state = lambda h: jnp.transpose(h, (1, 0, 2)).reshape(bsz, SSM_GROUPS, SSM_STATE)
    return y, state(h_re), state(h_im)


def _s5_step_weights(a_re, a_im, log_dt, b_re, b_im, c_re, c_im):
    _, _, ab_re, ab_im, bb_re, bb_im = _s5_discretise(a_re, a_im, log_dt, b_re, b_im)
    eye = jnp.eye(SSM_GROUPS, dtype=F32)
    n_state = SSM_GROUPS * SSM_STATE
    bfull = lambda bb: jnp.einsum("gpn,gh->gnhp", bb, eye).reshape(SSM_WIDTH, n_state)
    cfull = lambda cc: jnp.einsum("gmp,gh->hpgm", cc, eye).reshape(n_state, SSM_WIDTH)
    return (ab_re.reshape(1, n_state), ab_im.reshape(1, n_state), bfull(bb_re).astype(BF16), bfull(bb_im).astype(BF16),
            cfull(c_re).astype(BF16), cfull(c_im).astype(BF16))


def _s5_step_kernel(u_ref, h0re_ref, h0im_ref, are_ref, aim_ref, bre_ref, bim_ref, cre_ref, cim_ref,
                    y_ref, hre_ref, him_ref, *, n_new):
    h_re = h0re_ref[...]
    h_im = h0im_ref[...]
    a_re = are_ref[...]
    a_im = aim_ref[...]
    for t in range(n_new):
        ut = u_ref[t].astype(BF16)
        h_re, h_im = (a_re * h_re - a_im * h_im + _dot(ut, bre_ref[...]),
                      a_re * h_im + a_im * h_re + _dot(ut, bim_ref[...]))
        y_ref[t] = _dot(h_re.astype(BF16), cre_ref[...]) - _dot(h_im.astype(BF16), cim_ref[...])
    hre_ref[...] = h_re
    him_ref[...] = h_im


def _s5_step_call(u, h0_re, h0_im, weights, *, bsz, n_new):
    a_re, a_im, b_re, b_im, c_re, c_im = weights
    n_state = SSM_GROUPS * SSM_STATE
    u_t = jnp.swapaxes(u.reshape(bsz, n_new, SSM_WIDTH), 0, 1)
    y, h_re, h_im = pl.pallas_call(
        functools.partial(_s5_step_kernel, n_new=n_new),
        out_shape=[jax.ShapeDtypeStruct(u_t.shape, F32), jax.ShapeDtypeStruct((bsz, n_state), F32),
                   jax.ShapeDtypeStruct((bsz, n_state), F32)],
        compiler_params=pltpu.CompilerParams(vmem_limit_bytes=VMEM_LIMIT),
        name="s5_step",
    )(u_t, h0_re.reshape(bsz, n_state), h0_im.reshape(bsz, n_state), a_re, a_im, b_re, b_im, c_re, c_im)
    y = jnp.swapaxes(y, 0, 1).reshape(bsz * n_new, SSM_WIDTH)
    return y, h_re.reshape(bsz, SSM_GROUPS, SSM_STATE), h_im.reshape(bsz, SSM_GROUPS, SSM_STATE)


def _merge_kernel(x_ref, sh_ref, sc_ref, gt_ref, g1_ref, wg_ref, ys_ref, u_ref, d_ref, wglu_ref,
                  gated_ref, wgm_ref, att_ref, wat_ref, wo_ref, o_ref):
    x = x_ref[...]
    h = (_rms(x) * g1_ref[...] * (1.0 + sc_ref[0]) + sh_ref[0]).astype(BF16)
    y_s = ys_ref[...] + d_ref[...] * u_ref[...]
    glu = _dot(_gelu(y_s).astype(BF16), wglu_ref[...])
    d = D_MODEL
    merged = jax.nn.sigmoid(_dot(h, wg_ref[:, 0:d])) * (glu[:, :d] * jax.nn.sigmoid(glu[:, d:]))
    merged = merged + jax.nn.sigmoid(_dot(h, wg_ref[:, d:2 * d])) * _dot(gated_ref[...], wgm_ref[...])
    merged = merged + jax.nn.sigmoid(_dot(h, wg_ref[:, 2 * d:3 * d])) * _dot(att_ref[...], wat_ref[...])
    o_ref[...] = x + gt_ref[0] * _dot(merged.astype(BF16), wo_ref[...])


def _merge_call(x, ada, g1, wg, ys, u, dskip, wglu, gated, wgm, att, wat, wo, *, tm, tiles_per_batch):
    t = x.shape[0]
    row = lambda n: pl.BlockSpec((tm, n), lambda i: (i, 0))
    full = lambda arr: pl.BlockSpec(arr.shape, lambda i: (0,) * arr.ndim)
    return pl.pallas_call(
        _merge_kernel,
        grid=(t // tm,),
        in_specs=[row(D_MODEL), _mod_spec(ada, tiles_per_batch, 0), _mod_spec(ada, tiles_per_batch, 1),
                  _mod_spec(ada, tiles_per_batch, 2), full(g1), full(wg), row(SSM_WIDTH), row(SSM_WIDTH), full(dskip),
                  full(wglu), row(GMLP_WIDTH), full(wgm), row(ATT_WIDTH), full(wat), full(wo)],
        out_specs=row(D_MODEL),
        out_shape=jax.ShapeDtypeStruct(x.shape, F32),
        compiler_params=_cparams(("arbitrary",)),
        name="merge_out",
    )(x, ada, ada, ada, g1, wg, ys, u, dskip, wglu, gated, wgm, att, wat, wo)


FF_SPLIT = 4


def _mlp_kernel(x_ref, sh_ref, sc_ref, gt_ref, g2_ref, wup_ref, wdn_ref, gfin_ref, o_ref, y_ref):
    x = x_ref[...]
    h = (_rms(x) * g2_ref[...] * (1.0 + sc_ref[0]) + sh_ref[0]).astype(BF16)
    blk = D_FF // FF_SPLIT
    acc = None
    for c in range(FF_SPLIT):
        a = jnp.maximum(_dot(h, wup_ref[:, c * blk:(c + 1) * blk]), 0.0)
        part = _dot((a * a).astype(BF16), wdn_ref[c * blk:(c + 1) * blk, :])
        acc = part if acc is None else acc + part
    out = x + gt_ref[0] * acc
    o_ref[...] = out
    y_ref[...] = _rms(out) * gfin_ref[...]


def _mlp_call(x, ada, g2, wup, wdn, gfin, *, tm, tiles_per_batch):
    t = x.shape[0]
    row = lambda n: pl.BlockSpec((tm, n), lambda i: (i, 0))
    full = lambda arr: pl.BlockSpec(arr.shape, lambda i: (0,) * arr.ndim)
    return pl.pallas_call(
        _mlp_kernel,
        grid=(t // tm,),
        in_specs=[row(D_MODEL), _mod_spec(ada, tiles_per_batch, 3), _mod_spec(ada, tiles_per_batch, 4),
                  _mod_spec(ada, tiles_per_batch, 5), full(g2), full(wup), full(wdn), full(gfin)],
        out_specs=[row(D_MODEL), row(D_MODEL)],
        out_shape=[jax.ShapeDtypeStruct(x.shape, F32), jax.ShapeDtypeStruct(x.shape, F32)],
        compiler_params=_cparams(("arbitrary",)),
        name="mlp",
    )(x, ada, ada, ada, g2, wup, wdn, gfin)


def _layer_weights(l, w_in, b_f, w_s, b_s, mix_rows, seq_rows):
    wi = w_in[l]
    w_f = jnp.pad(wi[:, F_OFF:F_OFF + ATT_HEADS], ((0, 0), (0, LANE - ATT_HEADS)))
    w_main = jnp.concatenate([wi[:, Q_OFF:F_OFF], wi[:, S_OFF:G_OFF], w_f], axis=1).astype(BF16)
    w_gate = wi[:, G_OFF:].astype(BF16)
    bf = jnp.pad(b_f[l], (0, LANE - ATT_HEADS)).reshape(1, LANE)
    return w_main, w_gate, bf


def _mix_weights(w_s_l, b_s_l, seq, rows):
    cl = min(seq, CHUNK)
    w = jnp.tril(w_s_l[:, :cl, :cl])
    reps = rows // cl
    eye = jnp.eye(reps, dtype=w.dtype)
    wmix = jnp.einsum("gts,ab->gatbs", w, eye).reshape(GMLP_GROUPS, rows, rows).astype(BF16)
    bias = jnp.tile(jnp.transpose(b_s_l[:, :cl]), (reps, 1))
    bmix = jnp.repeat(bias, GMLP_GROUP_CH, axis=1)
    return wmix, bmix


def _cumsum_tri(rows, seq):
    r = jnp.arange(rows)
    return ((r[:, None] >= r[None, :]) & (r[:, None] // seq == r[None, :] // seq)).astype(BF16)


def kernel(x_prompt, x_sample, c_prompt, c_sample, cache_k, cache_v, cache_logf, state_ssm_re, state_ssm_im, page_table, w_ada, b_ada, g_norm1, w_in, b_f, ssm_a_re, ssm_a_im, ssm_log_dt, ssm_b_re, ssm_b_im, ssm_c_re, ssm_c_im, ssm_d, w_glu, g_gv, w_s, b_s, w_gmlp_out, w_att_out, w_o, g_norm2, w_up, w_down, g_final):
    depth = w_in.shape[0]
    bp, lp, d = x_prompt.shape
    bs, ls, _ = x_sample.shape
    tp, ts = bp * lp, bs * ls
    tm_p = 512
    tiles_pb = lp // tm_p

    ada = _ada_call(jnp.concatenate([c_prompt, c_sample], axis=0), w_ada, b_ada)
    n_pool, page = cache_k.shape[1], cache_k.shape[2]
    cache_k = cache_k.reshape(depth, n_pool, page, ATT_WIDTH)
    cache_v = cache_v.reshape(depth, n_pool, page, ATT_WIDTH)
    cache_lf_t = jnp.swapaxes(cache_logf, 2, 3)

    xp = x_prompt.reshape(tp, d)
    xs = x_sample.reshape(ts, d)
    tri_p = _cumsum_tri(tm_p, tm_p)
    tri_s = _cumsum_tri(ts, ls)
    row2 = lambda a: a.reshape(1, -1)
    outs = {k: [] for k in ("kp", "vp", "lfp", "hrp", "hip", "gvp", "ks", "vs", "lfs", "hrs", "his", "gvs")}
    yp = ys = None
    chunk_start = ((lp - 1) // CHUNK) * CHUNK
    for l in range(depth):
        w_main, w_gate, bf = _layer_weights(l, w_in, b_f, w_s, b_s, None, None)
        g1, g2, ggv, dskip = row2(g_norm1[l]), row2(g_norm2[l]), row2(g_gv[l]), row2(ssm_d[l])
        wglu, wgm, wat, wo = (w_glu[l].astype(BF16), w_gmlp_out[l].astype(BF16), w_att_out[l].astype(BF16),
                              w_o[l].astype(BF16))
        wup, wdn = w_up[l].astype(BF16), w_down[l].astype(BF16)
        gfin = row2(g_final)
        ssm = (ssm_a_re[l], ssm_a_im[l], ssm_log_dt[l], ssm_b_re[l], ssm_b_im[l], ssm_c_re[l], ssm_c_im[l])

        ada_p = ada[l, :bp].reshape(bp, 1, 6 * d)
        wmix, bmix = _mix_weights(w_s[l], b_s[l], lp, CHUNK)
        q, kf, kb, vf, vb, lf, cum, u, gated, gv = _in_proj_call(
            xp, ada_p, g1, w_main, bf, ggv, wmix, bmix, tri_p, tm=tm_p, tiles_per_batch=tiles_pb)
        cum_t = jnp.swapaxes(cum.reshape(bp, lp, ATT_HEADS), 1, 2)
        att = _flash_call(q, kb, vb, cum, cum_t, bsz=bp, seq=lp, tq=tm_p)
        y_ssm, hr, hi = _s5_chunk_call(u, _s5_chunk_weights(*ssm, lp // S5_CHUNK), bsz=bp, seq=lp)
        xp = _merge_call(xp, ada_p, g1, w_gate, y_ssm, u, dskip, wglu, gated, wgm, att, wat, wo,
                         tm=tm_p, tiles_per_batch=tiles_pb)
        xp, yp = _mlp_call(xp, ada_p, g2, wup, wdn, gfin, tm=tm_p, tiles_per_batch=tiles_pb)
        outs["kp"].append(kf.reshape(bp, lp, ATT_HEADS, HEAD_DIM))
        outs["vp"].append(vf.reshape(bp, lp, ATT_HEADS, HEAD_DIM))
        outs["lfp"].append(lf.reshape(bp, lp, ATT_HEADS))
        outs["hrp"].append(hr)
        outs["hip"].append(hi)
        outs["gvp"].append(gv.reshape(bp, lp, GMLP_WIDTH)[:, chunk_start:])

        ada_s = jnp.repeat(ada[l, bp:], ls, axis=0).reshape(1, ts, 6 * d)
        wmix, bmix = _mix_weights(w_s[l], b_s[l], ls, ts)
        q, kf, kb, vf, vb, lf, cum, u, gated, gv = _in_proj_call(
            xs, ada_s, g1, w_main, bf, ggv, wmix, bmix, tri_s, tm=ts, tiles_per_batch=1)
        cum_t = jnp.swapaxes(cum.reshape(bs, ls, ATT_HEADS), 1, 2)
        att = _decode_call(page_table, q, kb, vb, cum, cum_t, cache_k, cache_v, cache_lf_t,
                           layer=l, bsz=bs, n_new=ls)
        y_ssm, hr, hi = _s5_step_call(u, state_ssm_re[l], state_ssm_im[l], _s5_step_weights(*ssm), bsz=bs, n_new=ls)
        xs = _merge_call(xs, ada_s, g1, w_gate, y_ssm, u, dskip, wglu, gated, wgm, att, wat, wo,
                         tm=ts, tiles_per_batch=1)
        xs, ys = _mlp_call(xs, ada_s, g2, wup, wdn, gfin, tm=ts, tiles_per_batch=1)
        outs["ks"].append(kf.reshape(bs, ls, ATT_HEADS, HEAD_DIM))
        outs["vs"].append(vf.reshape(bs, ls, ATT_HEADS, HEAD_DIM))
        outs["lfs"].append(lf.reshape(bs, ls, ATT_HEADS))
        outs["hrs"].append(hr)
        outs["his"].append(hi)
        outs["gvs"].append(gv.reshape(bs, ls, GMLP_WIDTH))

    st = lambda k: jnp.stack(outs[k])
    return (yp.reshape(bp, lp, d), ys.reshape(bs, ls, d), st("kp"), st("vp"), st("lfp"), st("hrp"), st("hip"), st("gvp"),
            st("ks"), st("vs"), st("lfs"), st("hrs"), st("his"), st("gvs"))
```

```python
import functools
import math

import jax
import jax.numpy as jnp
from jax import lax
from jax.experimental import pallas as pl
from jax.experimental.pallas import tpu as pltpu

F32 = jnp.float32
BF16 = jnp.bfloat16

D_MODEL = 1024
ATT_HEADS = 8
HEAD_DIM = 64
ATT_WIDTH = ATT_HEADS * HEAD_DIM
SSM_GROUPS = 16
SSM_GROUP_CH = 16
SSM_WIDTH = SSM_GROUPS * SSM_GROUP_CH
SSM_STATE = 64
N_STATE = SSM_GROUPS * SSM_STATE
GMLP_GROUPS = 4
GMLP_GROUP_CH = 64
GMLP_WIDTH = GMLP_GROUPS * GMLP_GROUP_CH
CHUNK = 128
N_BRANCH = 3
D_FF = 4 * D_MODEL
EPS = 1e-6

Q_OFF = 0
K_OFF = Q_OFF + ATT_WIDTH
V_OFF = K_OFF + ATT_WIDTH
F_OFF = V_OFF + ATT_WIDTH
S_OFF = F_OFF + ATT_HEADS
GU_OFF = S_OFF + SSM_WIDTH
GV_OFF = GU_OFF + GMLP_WIDTH
G_OFF = GV_OFF + GMLP_WIDTH

LOG2E = math.log2(math.e)
LANE = 128
BF16_ROWS = 16
S5_CHUNK = 16
TM_PROMPT = 512
DEC_PAGES = 8
FF_SPLIT = 4
VMEM_LIMIT = 56 * 1024 * 1024


def _cparams(sem):
    return pltpu.CompilerParams(dimension_semantics=sem, vmem_limit_bytes=VMEM_LIMIT)


def _dot(a, b):
    return jnp.dot(a, b, preferred_element_type=F32)


def _dot_nt(a, b):
    return lax.dot_general(a, b, (((1,), (1,)), ((), ())), preferred_element_type=F32)


def _dot_tn(a, b):
    return lax.dot_general(a, b, (((0,), (0,)), ((), ())), preferred_element_type=F32)


def _split3(x):
    hi = x.astype(BF16)
    r1 = x - hi.astype(F32)
    mid = r1.astype(BF16)
    lo = (r1 - mid.astype(F32)).astype(BF16)
    return hi, mid, lo


def _gelu(x):
    return 0.5 * x * (1.0 + jnp.tanh(math.sqrt(2.0 / math.pi) * (x + 0.044715 * (x * x * x))))


def _log_sigmoid(x):
    return -(jnp.maximum(-x, 0.0) + jnp.log1p(jnp.exp(-jnp.abs(x))))


def _rms(x):
    return x * lax.rsqrt(jnp.mean(x * x, axis=-1, keepdims=True) + EPS)


def _full(arr):
    return pl.BlockSpec(arr.shape, lambda *_: (0,) * arr.ndim)


def _ada_kernel(c_ref, w_ref, b_ref, o_ref):
    c = c_ref[...]
    a = (c * jax.nn.sigmoid(c)).astype(BF16)
    o_ref[0] = _dot(a, w_ref[0].astype(BF16)) + b_ref[0]


def _ada_call(c_all, w_ada, b_ada):
    depth, d, n = w_ada.shape
    r = c_all.shape[0]
    tn = 1536
    return pl.pallas_call(
        _ada_kernel,
        grid=(depth, n // tn),
        in_specs=[pl.BlockSpec((r, d), lambda l, j: (0, 0)),
                  pl.BlockSpec((1, d, tn), lambda l, j: (l, 0, j)),
                  pl.BlockSpec((1, 1, tn), lambda l, j: (l, 0, j))],
        out_specs=pl.BlockSpec((1, r, tn), lambda l, j: (l, 0, j)),
        out_shape=jax.ShapeDtypeStruct((depth, r, n), F32),
        compiler_params=_cparams(("arbitrary", "arbitrary")),
        name="ada",
    )(c_all, w_ada, b_ada.reshape(depth, 1, n))


def _mod_spec(ada, tiles_per_batch, j):
    r = ada.shape[1]
    return pl.BlockSpec((1, r, D_MODEL), lambda i: (i // tiles_per_batch, 0, j))


def _gmlp_gate(gu, gv, wmix_ref, bmix_ref, gated_ref, mix_rows):
    lane = lax.broadcasted_iota(jnp.int32, (1, GMLP_WIDTH), 1)
    for c in range(gu.shape[0] // mix_rows):
        rows = slice(c * mix_rows, (c + 1) * mix_rows)
        gvc = gv[rows]
        s = bmix_ref[...]
        for g in range(GMLP_GROUPS):
            in_g = (lane >= g * GMLP_GROUP_CH) & (lane < (g + 1) * GMLP_GROUP_CH)
            s = s + _dot(wmix_ref[g], jnp.where(in_g, gvc, 0.0).astype(BF16))
        gated_ref[rows, :] = (gu[rows] * s).astype(BF16)


W_ROW_COLS = ATT_WIDTH + SSM_WIDTH + 2 * GMLP_WIDTH + LANE
W_T_ROWS = 3 * ATT_WIDTH + BF16_ROWS


def _in_proj_p_kernel(x_ref, sh_ref, sc_ref, g1_ref, wr_ref, wt_ref, bfr_ref, bfc_ref, ggv_ref, wmix_ref, bmix_ref,
                      tri_ref, triu_ref,
                      qt_ref, ktf_ref, vtf_ref, vtb_ref, kb_ref, lft_ref, cumt_ref, fkb_ref, ulo_ref, uhi_ref,
                      gated_ref, gv_ref, carry_r, carry_c, *, tiles_per_batch, mix_rows):
    i = pl.program_id(0)
    tm = x_ref.shape[0]
    a = ATT_WIDTH
    h = (_rms(x_ref[...]) * g1_ref[...] * (1.0 + sc_ref[0]) + sh_ref[0]).astype(BF16)

    @pl.when(i % tiles_per_batch == 0)
    def _():
        carry_r[...] = jnp.zeros_like(carry_r)
        carry_c[...] = jnp.zeros_like(carry_c)

    zt = _dot_nt(wt_ref[...], h)
    qt_ref[0] = (zt[0:a] * (LOG2E * HEAD_DIM ** -0.5)).astype(BF16)
    ktf_ref[0] = zt[a:2 * a]
    vt = zt[2 * a:3 * a]
    vtf_ref[0] = vt
    vtb_ref[0] = vt.astype(BF16)
    lft = _log_sigmoid(zt[3 * a:3 * a + ATT_HEADS] + bfc_ref[...])
    lft_ref[0] = lft
    hi, mid, lo = _split3(lft)
    triu = triu_ref[...]
    cumt = (_dot(hi, triu) + _dot(mid, triu)) + _dot(lo, triu) + carry_c[...]
    cumt_ref[0] = cumt * LOG2E
    carry_c[...] = cumt[:, tm - 1:tm]

    kb_ref[...] = _dot(h, wr_ref[:, 0:a]).astype(BF16)
    o = a
    u = _dot(h, wr_ref[:, o:o + SSM_WIDTH])
    ulo_ref[...] = u[:, :LANE]
    uhi_ref[...] = u[:, LANE:]
    o += SSM_WIDTH
    gu = _gelu(_dot(h, wr_ref[:, o:o + GMLP_WIDTH]))
    o += GMLP_WIDTH
    gv = _rms(_gelu(_dot(h, wr_ref[:, o:o + GMLP_WIDTH]))) * ggv_ref[...]
    gv_ref[...] = gv
    o += GMLP_WIDTH

    lf = _log_sigmoid(_dot(h, wr_ref[:, o:o + LANE]) + bfr_ref[...])
    hi, mid, lo = _split3(lf)
    tri = tri_ref[...]
    cum = (_dot(tri, hi) + _dot(tri, mid)) + _dot(tri, lo) + carry_r[...]
    carry_r[...] = cum[tm - 1:tm, :]
    for hh in range(ATT_HEADS):
        fkb_ref[0, hh] = jnp.broadcast_to(cum[:, hh:hh + 1] * LOG2E, (tm, LANE))

    _gmlp_gate(gu, gv, wmix_ref, bmix_ref, gated_ref, mix_rows)


def _in_proj_p_call(x, ada, g1, wr, wt, bfr, bfc, ggv, wmix, bmix, tri, triu, *, bsz, seq, tm):
    t = x.shape[0]
    tpb = seq // tm
    row = lambda n: pl.BlockSpec((tm, n), lambda i: (i, 0))
    fm = lambda n: pl.BlockSpec((1, n, tm), lambda i: (i // tpb, 0, i % tpb))
    outs = [(fm(ATT_WIDTH), (bsz, ATT_WIDTH, seq), BF16),
            (fm(ATT_WIDTH), (bsz, ATT_WIDTH, seq), F32),
            (fm(ATT_WIDTH), (bsz, ATT_WIDTH, seq), F32),
            (fm(ATT_WIDTH), (bsz, ATT_WIDTH, seq), BF16),
            (row(ATT_WIDTH), (t, ATT_WIDTH), BF16),
            (fm(ATT_HEADS), (bsz, ATT_HEADS, seq), F32),
            (fm(ATT_HEADS), (bsz, ATT_HEADS, seq), F32),
            (pl.BlockSpec((1, ATT_HEADS, tm, LANE), lambda i: (i // tpb, 0, i % tpb, 0)),
             (bsz, ATT_HEADS, seq, LANE), F32),
            (row(LANE), (t, LANE), F32),
            (row(LANE), (t, LANE), F32),
            (row(GMLP_WIDTH), (t, GMLP_WIDTH), BF16),
            (row(GMLP_WIDTH), (t, GMLP_WIDTH), F32)]
    return pl.pallas_call(
        functools.partial(_in_proj_p_kernel, tiles_per_batch=tpb, mix_rows=wmix.shape[1]),
        grid=(t // tm,),
        in_specs=[row(D_MODEL), _mod_spec(ada, tpb, 0), _mod_spec(ada, tpb, 1), _full(g1), _full(wr), _full(wt),
                  _full(bfr), _full(bfc), _full(ggv), _full(wmix), _full(bmix), _full(tri), _full(triu)],
        out_specs=[o[0] for o in outs],
        out_shape=[jax.ShapeDtypeStruct(o[1], o[2]) for o in outs],
        scratch_shapes=[pltpu.VMEM((1, LANE), F32), pltpu.VMEM((ATT_HEADS, 1), F32)],
        compiler_params=_cparams(("arbitrary",)),
        name="in_proj_p",
    )(x, ada, ada, g1, wr, wt, bfr, bfc, ggv, wmix, bmix, tri, triu)


W_S_COLS = 3 * ATT_WIDTH + SSM_WIDTH + 2 * GMLP_WIDTH + LANE


def _in_proj_s_kernel(x_ref, sh_ref, sc_ref, g1_ref, w_ref, bf_ref, ggv_ref, wmix_ref, bmix_ref, tri_ref,
                      q_ref, kf_ref, kb_ref, vf_ref, vb_ref, lf_ref, cum_ref, u_ref, gated_ref, gv_ref, *, mix_rows):
    h = (_rms(x_ref[...]) * g1_ref[...] * (1.0 + sc_ref[0]) + sh_ref[0]).astype(BF16)
    a = ATT_WIDTH
    q_ref[...] = (_dot(h, w_ref[:, 0:a]) * (HEAD_DIM ** -0.5)).astype(BF16)
    k = _dot(h, w_ref[:, a:2 * a])
    kf_ref[...] = k
    kb_ref[...] = k.astype(BF16)
    v = _dot(h, w_ref[:, 2 * a:3 * a])
    vf_ref[...] = v
    vb_ref[...] = v.astype(BF16)
    o = 3 * a
    u_ref[...] = _dot(h, w_ref[:, o:o + SSM_WIDTH])
    o += SSM_WIDTH
    gu = _gelu(_dot(h, w_ref[:, o:o + GMLP_WIDTH]))
    o += GMLP_WIDTH
    gv = _rms(_gelu(_dot(h, w_ref[:, o:o + GMLP_WIDTH]))) * ggv_ref[...]
    gv_ref[...] = gv
    o += GMLP_WIDTH
    lf = _log_sigmoid(_dot(h, w_ref[:, o:o + LANE]) + bf_ref[...])
    lf_ref[...] = lf[:, :ATT_HEADS]
    hi, mid, lo = _split3(lf)
    tri = tri_ref[...]
    cum = (_dot(tri, hi) + _dot(tri, mid)) + _dot(tri, lo)
    cum_ref[...] = cum[:, :ATT_HEADS]
    _gmlp_gate(gu, gv, wmix_ref, bmix_ref, gated_ref, mix_rows)


def _in_proj_s_call(x, ada, g1, w, bf, ggv, wmix, bmix, tri):
    t = x.shape[0]
    row = lambda n: pl.BlockSpec((t, n), lambda i: (0, 0))
    outs = [(ATT_WIDTH, BF16), (ATT_WIDTH, F32), (ATT_WIDTH, BF16), (ATT_WIDTH, F32), (ATT_WIDTH, BF16),
            (ATT_HEADS, F32), (ATT_HEADS, F32), (SSM_WIDTH, F32), (GMLP_WIDTH, BF16), (GMLP_WIDTH, F32)]
    return pl.pallas_call(
        functools.partial(_in_proj_s_kernel, mix_rows=wmix.shape[1]),
        grid=(1,),
        in_specs=[row(D_MODEL), _mod_spec(ada, 1, 0), _mod_spec(ada, 1, 1),
                  _full(g1), _full(w), _full(bf), _full(ggv), _full(wmix), _full(bmix), _full(tri)],
        out_specs=[row(n) for n, _ in outs],
        out_shape=[jax.ShapeDtypeStruct((t, n), dt) for n, dt in outs],
        compiler_params=_cparams(("arbitrary",)),
        name="in_proj_s",
    )(x, ada, ada, g1, w, bf, ggv, wmix, bmix, tri)


def _flash_kernel(qi_tab, ki_tab, k_ref, qt_ref, vt_ref, fkb_ref, fq_ref, o_ref, m_sc, l_sc, acc_sc):
    hp = pl.program_id(1)
    s_idx = pl.program_id(2)
    qi = qi_tab[s_idx]
    ki = ki_tab[s_idx]
    tk = k_ref.shape[0]
    tq = qt_ref.shape[2]

    @pl.when(ki == 0)
    def _():
        m_sc[...] = jnp.full_like(m_sc, -jnp.inf)
        l_sc[...] = jnp.zeros_like(l_sc)
        acc_sc[...] = jnp.zeros_like(acc_sc)

    def step(masked):
        k = k_ref[...]
        qt = qt_ref[0]
        vt = vt_ref[0]
        feat = lax.broadcasted_iota(jnp.int32, (2 * HEAD_DIM, 1), 0)
        for hh in range(2):
            own = (feat >= hh * HEAD_DIM) & (feat < (hh + 1) * HEAD_DIM)
            st = _dot(k, jnp.where(own, qt, jnp.zeros_like(qt)))
            fq = fq_ref[0, pl.ds(2 * hp + hh, 1), :]
            st = st - jnp.concatenate([fkb_ref[0, hh]] * (tq // LANE), axis=1)
            if masked:
                kj = lax.broadcasted_iota(jnp.int32, (tk, tq), 0)
                qc = lax.broadcasted_iota(jnp.int32, (tk, tq), 1)
                st = jnp.where(kj <= qc, st, -jnp.inf)
            m_prev = m_sc[hh]
            m_new = jnp.maximum(m_prev, fq + jnp.max(st, axis=0, keepdims=True))
            alpha = jnp.exp2(m_prev - m_new)
            p = jnp.exp2(st + (fq - m_new))
            l_sc[hh] = alpha * l_sc[hh] + jnp.sum(p, axis=0, keepdims=True)
            acc_sc[hh] = alpha * acc_sc[hh] + _dot(vt[hh * HEAD_DIM:(hh + 1) * HEAD_DIM], p.astype(BF16))
            m_sc[hh] = m_new

    @pl.when(ki < qi)
    def _():
        step(False)

    @pl.when(ki == qi)
    def _():
        step(True)
        o_ref[0] = jnp.concatenate([acc_sc[0] / l_sc[0], acc_sc[1] / l_sc[1]], axis=0).astype(o_ref.dtype)


def _flash_call(k, qt, vt, fkb, cumt, *, bsz, seq, tq):
    nq = seq // tq
    tri = [(a, b) for a in range(nq) for b in range(a + 1)]
    qi_tab = jnp.array([a for a, _ in tri], jnp.int32)
    ki_tab = jnp.array([b for _, b in tri], jnp.int32)
    pair = 2 * HEAD_DIM
    q_spec = pl.BlockSpec((1, pair, tq), lambda b, hp, s, qt_, kt_: (b, hp, qt_[s]))
    return pl.pallas_call(
        _flash_kernel,
        grid_spec=pltpu.PrefetchScalarGridSpec(
            num_scalar_prefetch=2,
            grid=(bsz, ATT_HEADS // 2, len(tri)),
            in_specs=[pl.BlockSpec((tq, pair), lambda b, hp, s, qt_, kt_: (b * nq + kt_[s], hp)),
                      q_spec,
                      pl.BlockSpec((1, pair, tq), lambda b, hp, s, qt_, kt_: (b, hp, kt_[s])),
                      pl.BlockSpec((1, 2, tq, LANE), lambda b, hp, s, qt_, kt_: (b, hp, kt_[s], 0)),
                      pl.BlockSpec((1, ATT_HEADS, tq), lambda b, hp, s, qt_, kt_: (b, 0, qt_[s]))],
            out_specs=q_spec,
            scratch_shapes=[pltpu.VMEM((2, 1, tq), F32), pltpu.VMEM((2, 1, tq), F32),
                            pltpu.VMEM((2, HEAD_DIM, tq), F32)]),
        out_shape=jax.ShapeDtypeStruct(qt.shape, BF16),
        compiler_params=_cparams(("arbitrary", "arbitrary", "arbitrary")),
        name="flash",
    )(qi_tab, ki_tab, k, qt, vt, fkb, cumt)


def _decode_kernel(pt_ref, q_ref, kn_ref, vn_ref, cn_ref, cnt_ref, hmask_ref, qmask_ref, nmask_ref, tri_ref,
                   ck_hbm, cv_hbm, clf_hbm, o_ref, kbuf, vbuf, lbuf, sem, *, layer, n_pages):
    b = pl.program_id(0)
    nb = pl.num_programs(0)
    page = lbuf.shape[3]
    n_chunks = n_pages // DEC_PAGES
    n_q = q_ref.shape[0]
    rows = n_q * ATT_HEADS

    def copies(bb, chunk, slot):
        out = []
        for j in range(DEC_PAGES):
            pg = pt_ref[bb, chunk * DEC_PAGES + j]
            cols = pl.ds(j * page, page)
            out.append(pltpu.make_async_copy(ck_hbm.at[layer, pg], kbuf.at[slot, :, cols], sem.at[slot, 0]))
            out.append(pltpu.make_async_copy(cv_hbm.at[layer, pg], vbuf.at[slot, :, cols], sem.at[slot, 1]))
            out.append(pltpu.make_async_copy(clf_hbm.at[layer, pg], lbuf.at[slot, j], sem.at[slot, 2]))
        return out

    def chunk_of(step):
        return n_chunks - 1 - step

    @pl.when(b == 0)
    def _():
        for cp in copies(b, chunk_of(0), 0):
            cp.start()

    q = q_ref[...].astype(F32)
    qbd = jnp.broadcast_to(q[:, None, :], (n_q, ATT_HEADS, ATT_WIDTH)).reshape(rows, ATT_WIDTH)
    qbd = jnp.where(hmask_ref[...] > 0, qbd, 0.0).astype(BF16)
    cn = cn_ref[...]
    cn_rows = jnp.broadcast_to(cn[:, None, :], (n_q, ATT_HEADS, ATT_HEADS)).reshape(rows, ATT_HEADS)
    fq = jnp.sum(jnp.where(qmask_ref[...] > 0, cn_rows, 0.0), axis=-1, keepdims=True)

    s = _dot_nt(qbd, kn_ref[...])
    fk_new = jnp.concatenate([cnt_ref[0]] * n_q, axis=0)
    s = jnp.where(nmask_ref[...] > 0, s + fq - fk_new, -jnp.inf)
    m = jnp.max(s, axis=-1, keepdims=True)
    p = jnp.exp(s - m)
    l = jnp.sum(p, axis=-1, keepdims=True)
    acc = _dot(p.astype(BF16), vn_ref[...])
    run = jnp.zeros((ATT_HEADS, 1), F32)

    for step in range(n_chunks):
        slot = step % 2
        for cp in copies(b, chunk_of(step), slot):
            cp.wait()
        if step + 1 < n_chunks:
            for cp in copies(b, chunk_of(step + 1), 1 - slot):
                cp.start()
        else:
            @pl.when(b + 1 < nb)
            def _():
                for cp in copies(b + 1, chunk_of(0), 1 - slot):
                    cp.start()

        kc = kbuf[slot].astype(BF16)
        vc = vbuf[slot].astype(BF16)
        lf = lbuf[slot].reshape(DEC_PAGES * ATT_HEADS, page)
        hi, mid, lo = _split3(lf)
        tri = tri_ref[...]
        suf = (_dot(hi, tri) + _dot(mid, tri)) + _dot(lo, tri)
        tot = jnp.sum(lf, axis=-1, keepdims=True)
        bias = []
        for j in reversed(range(DEC_PAGES)):
            pr = slice(j * ATT_HEADS, (j + 1) * ATT_HEADS)
            bias.append(suf[pr] + run)
            run = run + tot[pr]
        bias = jnp.concatenate(bias[::-1], axis=1)
        s = _dot(qbd, kc) + fq + jnp.concatenate([bias] * n_q, axis=0)
        m_new = jnp.maximum(m, jnp.max(s, axis=-1, keepdims=True))
        alpha = jnp.exp(m - m_new)
        p = jnp.exp(s - m_new)
        l = alpha * l + jnp.sum(p, axis=-1, keepdims=True)
        acc = alpha * acc + _dot_nt(p.astype(BF16), vc)
        m = m_new

    out = jnp.where(hmask_ref[...] > 0, acc / l, 0.0)
    o_ref[...] = jnp.sum(out.reshape(n_q, ATT_HEADS, ATT_WIDTH), axis=1).astype(o_ref.dtype)


def _decode_call(page_table, q, k_new, v_new, cn, cn_t, cache_kt, cache_vt, cache_lf_t, *, layer, bsz, n_new):
    n_pages = page_table.shape[1]
    page = cache_kt.shape[3]
    rows = n_new * ATT_HEADS
    r = jnp.arange(rows)
    hmask = (r[:, None] % ATT_HEADS == jnp.arange(ATT_WIDTH)[None, :] // HEAD_DIM).astype(F32)
    qmask = (r[:, None] % ATT_HEADS == jnp.arange(ATT_HEADS)[None, :]).astype(F32)
    nmask = (r[:, None] // ATT_HEADS >= jnp.arange(n_new)[None, :]).astype(F32)
    tri = (jnp.arange(page)[:, None] > jnp.arange(page)[None, :]).astype(BF16)
    tok = lambda n: pl.BlockSpec((n_new, n), lambda b, pt: (b, 0))
    hbm = pl.BlockSpec(memory_space=pl.ANY)
    return pl.pallas_call(
        functools.partial(_decode_kernel, layer=layer, n_pages=n_pages),
        grid_spec=pltpu.PrefetchScalarGridSpec(
            num_scalar_prefetch=1,
            grid=(bsz,),
            in_specs=[tok(ATT_WIDTH), tok(ATT_WIDTH), tok(ATT_WIDTH), tok(ATT_HEADS),
                      pl.BlockSpec((1, ATT_HEADS, n_new), lambda b, pt: (b, 0, 0)),
                      _full(hmask), _full(qmask), _full(nmask), _full(tri), hbm, hbm, hbm],
            out_specs=tok(ATT_WIDTH),
            scratch_shapes=[pltpu.VMEM((2, ATT_WIDTH, DEC_PAGES * page), F32),
                            pltpu.VMEM((2, ATT_WIDTH, DEC_PAGES * page), F32),
                            pltpu.VMEM((2, DEC_PAGES, ATT_HEADS, page), F32),
                            pltpu.SemaphoreType.DMA((2, 3))]),
        out_shape=jax.ShapeDtypeStruct(q.shape, BF16),
        compiler_params=_cparams(("arbitrary",)),
        name="decode",
    )(page_table, q, k_new, v_new, cn, cn_t, hmask, qmask, nmask, tri, cache_kt, cache_vt, cache_lf_t)


def _s5_discretise(a_re, a_im, log_dt, b_re, b_im):
    dt = jnp.exp(log_dt)[:, None]
    rate = a_re * dt
    ang = a_im * dt
    mag = jnp.exp(rate)
    ab_re = mag * jnp.cos(ang)
    ab_im = mag * jnp.sin(ang)
    den = a_re * a_re + a_im * a_im
    z_re = ((ab_re - 1.0) * a_re + ab_im * a_im) / den
    z_im = (ab_im * a_re - (ab_re - 1.0) * a_im) / den
    bb_re = z_re[..., None] * b_re - z_im[..., None] * b_im
    bb_im = z_re[..., None] * b_im + z_im[..., None] * b_re
    return rate, ang, ab_re, ab_im, bb_re, bb_im


def _abar_pow(rate, ang, j):
    mag = jnp.exp(rate * j)
    return mag * jnp.cos(ang * j), mag * jnp.sin(ang * j)


def _group_diag(m):
    g, r, c = m.shape
    return jnp.einsum("grc,gh->grhc", m, jnp.eye(g, dtype=m.dtype)).reshape(g * r, g * c)


def _s5_weights(a_re, a_im, log_dt, b_re, b_im, c_re, c_im, chunks_per_seq):
    hp = lax.Precision.HIGHEST
    rate, ang, ab_re, ab_im, bb_re, bb_im = _s5_discretise(a_re, a_im, log_dt, b_re, b_im)
    c = S5_CHUNK
    bfull = jnp.concatenate([_group_diag(jnp.swapaxes(bb_re, 1, 2)), _group_diag(jnp.swapaxes(bb_im, 1, 2))], axis=1)
    cfull = jnp.concatenate([_group_diag(jnp.swapaxes(c_re, 1, 2)), -_group_diag(jnp.swapaxes(c_im, 1, 2))], axis=0)
    j = jnp.arange(c, dtype=F32)[:, None, None]
    p_re, p_im = _abar_pow(rate[None], ang[None], j)
    cb_re = c_re[None] * p_re[:, :, None, :] - c_im[None] * p_im[:, :, None, :]
    cb_im = c_re[None] * p_im[:, :, None, :] + c_im[None] * p_re[:, :, None, :]
    kern = jnp.einsum("jgmp,gpn->jgnm", cb_re, bb_re, precision=hp) - jnp.einsum("jgmp,gpn->jgnm", cb_im, bb_im, precision=hp)
    kst = jnp.concatenate([_group_diag(kern[c - 1 - s]) for s in range(c)], axis=0)
    n_steps = max(1, (chunks_per_seq - 1).bit_length())
    d = (c * 2.0 ** jnp.arange(n_steps, dtype=F32))[:, None, None]
    s_re, s_im = _abar_pow(rate[None], ang[None], d)
    flat = lambda a: a.reshape(a.shape[0], 1, N_STATE)
    return dict(a_re=ab_re.reshape(1, N_STATE), a_im=ab_im.reshape(1, N_STATE), b=bfull.astype(BF16),
                c=cfull.astype(BF16), kst=kst.astype(BF16), s_re=flat(s_re), s_im=flat(s_im))


def _s5_seq_kernel(ulo_ref, uhi_ref, d_ref, b_ref, c_ref, kst_ref, are_ref, aim_ref, sre_ref, sim_ref,
                   ylo_ref, yhi_ref, hre_ref, him_ref, u2_sc):
    w = SSM_WIDTH
    n = ulo_ref.shape[0] // S5_CHUNK
    a_re = are_ref[...]
    a_im = aim_ref[...]
    token = lambda ref, s: ref[pl.ds(s, n, stride=S5_CHUNK), :]
    h_re = h_im = None
    for s in range(S5_CHUNK):
        ub = jnp.concatenate([token(ulo_ref, s), token(uhi_ref, s)], axis=1).astype(BF16)
        u2_sc[:, s * w:(s + 1) * w] = ub
        x = _dot(ub, b_ref[...])
        x_re, x_im = x[:, :N_STATE], x[:, N_STATE:]
        if s == 0:
            h_re, h_im = x_re, x_im
        else:
            h_re, h_im = a_re * h_re - a_im * h_im + x_re, a_re * h_im + a_im * h_re + x_im
    pos = lax.broadcasted_iota(jnp.int32, (n, 1), 0)
    for k in range(sre_ref.shape[0]):
        d = 1 << k
        keep = pos >= d
        p_re = jnp.where(keep, pltpu.roll(h_re, d, 0), 0.0)
        p_im = jnp.where(keep, pltpu.roll(h_im, d, 0), 0.0)
        s_re = sre_ref[k]
        s_im = sim_ref[k]
        h_re, h_im = h_re + (s_re * p_re - s_im * p_im), h_im + (s_re * p_im + s_im * p_re)
    hre_ref[0] = h_re[n - 1:n]
    him_ref[0] = h_im[n - 1:n]
    keep = pos >= 1
    g_re = jnp.where(keep, pltpu.roll(h_re, 1, 0), 0.0)
    g_im = jnp.where(keep, pltpu.roll(h_im, 1, 0), 0.0)
    for t in range(S5_CHUNK):
        g_re, g_im = a_re * g_re - a_im * g_im, a_re * g_im + a_im * g_re
        g = jnp.concatenate([g_re, g_im], axis=1).astype(BF16)
        y = _dot(u2_sc[:, 0:(t + 1) * w], kst_ref[(S5_CHUNK - 1 - t) * w:, :]) + _dot(g, c_ref[...])
        ylo_ref[pl.ds(t, n, stride=S5_CHUNK), :] = y[:, :LANE] + d_ref[:, :LANE] * token(ulo_ref, t)
        yhi_ref[pl.ds(t, n, stride=S5_CHUNK), :] = y[:, LANE:] + d_ref[:, LANE:] * token(uhi_ref, t)


def _s5_seq_call(u_lo, u_hi, dskip, wts, *, bsz, seq):
    cps = seq // S5_CHUNK
    names = ("b", "c", "kst", "a_re", "a_im", "s_re", "s_im")
    half = pl.BlockSpec((seq, LANE), lambda i: (i, 0))
    state_spec = pl.BlockSpec((1, 1, N_STATE), lambda i: (i, 0, 0))
    y_lo, y_hi, h_re, h_im = pl.pallas_call(
        _s5_seq_kernel,
        grid=(bsz,),
        in_specs=[half, half, _full(dskip)] + [_full(wts[k]) for k in names],
        out_specs=[half, half, state_spec, state_spec],
        out_shape=[jax.ShapeDtypeStruct(u_lo.shape, F32), jax.ShapeDtypeStruct(u_lo.shape, F32),
                   jax.ShapeDtypeStruct((bsz, 1, N_STATE), F32), jax.ShapeDtypeStruct((bsz, 1, N_STATE), F32)],
        scratch_shapes=[pltpu.VMEM((cps, S5_CHUNK * SSM_WIDTH), BF16)],
        compiler_params=_cparams(("arbitrary",)),
        name="s5_seq",
    )(u_lo, u_hi, dskip, *[wts[k] for k in names])
    state = lambda h: h.reshape(bsz, SSM_GROUPS, SSM_STATE)
    return y_lo, y_hi, state(h_re), state(h_im)


def _s5_step_kernel(u_ref, d_ref, h0re_ref, h0im_ref, are_ref, aim_ref, b_ref, c_ref, y_ref, hre_ref, him_ref, *,
                    n_new):
    h_re = h0re_ref[...]
    h_im = h0im_ref[...]
    a_re = are_ref[...]
    a_im = aim_ref[...]
    for t in range(n_new):
        ut = u_ref[t]
        x = _dot(ut.astype(BF16), b_ref[...])
        h_re, h_im = (a_re * h_re - a_im * h_im + x[:, :N_STATE], a_re * h_im + a_im * h_re + x[:, N_STATE:])
        y_ref[t] = _dot(jnp.concatenate([h_re, h_im], axis=1).astype(BF16), c_ref[...]) + d_ref[...] * ut
    hre_ref[...] = h_re
    him_ref[...] = h_im


def _s5_step_call(u, dskip, h0_re, h0_im, wts, *, bsz, n_new):
    u_t = jnp.swapaxes(u.reshape(bsz, n_new, SSM_WIDTH), 0, 1)
    y, h_re, h_im = pl.pallas_call(
        functools.partial(_s5_step_kernel, n_new=n_new),
        out_shape=[jax.ShapeDtypeStruct(u_t.shape, F32), jax.ShapeDtypeStruct((bsz, N_STATE), F32),
                   jax.ShapeDtypeStruct((bsz, N_STATE), F32)],
        compiler_params=pltpu.CompilerParams(vmem_limit_bytes=VMEM_LIMIT),
        name="s5_step",
    )(u_t, dskip, h0_re.reshape(bsz, N_STATE), h0_im.reshape(bsz, N_STATE), wts["a_re"], wts["a_im"], wts["b"],
      wts["c"])
    y = jnp.swapaxes(y, 0, 1).reshape(bsz * n_new, SSM_WIDTH)
    return (y[:, :LANE], y[:, LANE:], h_re.reshape(bsz, SSM_GROUPS, SSM_STATE),
            h_im.reshape(bsz, SSM_GROUPS, SSM_STATE))


def _merge_kernel(x_ref, sh_ref, sc_ref, gt_ref, g1_ref, wg_ref, yslo_ref, yshi_ref, wglu_ref,
                  gated_ref, wgm_ref, att_ref, wat_ref, wo_ref, o_ref, *, att_feature_major):
    x = x_ref[...]
    h = (_rms(x) * g1_ref[...] * (1.0 + sc_ref[0]) + sh_ref[0]).astype(BF16)
    y_s = jnp.concatenate([yslo_ref[...], yshi_ref[...]], axis=1)
    glu = _dot(_gelu(y_s).astype(BF16), wglu_ref[...])
    d = D_MODEL
    y_att = _dot_tn(att_ref[0], wat_ref[...]) if att_feature_major else _dot(att_ref[...], wat_ref[...])
    merged = jax.nn.sigmoid(_dot(h, wg_ref[:, 0:d])) * (glu[:, :d] * jax.nn.sigmoid(glu[:, d:]))
    merged = merged + jax.nn.sigmoid(_dot(h, wg_ref[:, d:2 * d])) * _dot(gated_ref[...], wgm_ref[...])
    merged = merged + jax.nn.sigmoid(_dot(h, wg_ref[:, 2 * d:3 * d])) * y_att
    o_ref[...] = x + gt_ref[0] * _dot(merged.astype(BF16), wo_ref[...])


def _merge_call(x, ada, g1, wg, ys_lo, ys_hi, wglu, gated, wgm, att, wat, wo, *, tm, tiles_per_batch):
    t = x.shape[0]
    row = lambda n: pl.BlockSpec((tm, n), lambda i: (i, 0))
    fm = att.ndim == 3
    att_spec = (pl.BlockSpec((1, ATT_WIDTH, tm), lambda i: (i // tiles_per_batch, 0, i % tiles_per_batch))
                if fm else row(ATT_WIDTH))
    return pl.pallas_call(
        functools.partial(_merge_kernel, att_feature_major=fm),
        grid=(t // tm,),
        in_specs=[row(D_MODEL), _mod_spec(ada, tiles_per_batch, 0), _mod_spec(ada, tiles_per_batch, 1),
                  _mod_spec(ada, tiles_per_batch, 2), _full(g1), _full(wg), row(LANE), row(LANE),
                  _full(wglu), row(GMLP_WIDTH), _full(wgm), att_spec, _full(wat), _full(wo)],
        out_specs=row(D_MODEL),
        out_shape=jax.ShapeDtypeStruct(x.shape, F32),
        compiler_params=_cparams(("arbitrary",)),
        name="merge_out",
    )(x, ada, ada, ada, g1, wg, ys_lo, ys_hi, wglu, gated, wgm, att, wat, wo)


def _mlp_kernel(x_ref, sh_ref, sc_ref, gt_ref, g2_ref, wup_ref, wdn_ref, gfin_ref, o_ref, y_ref):
    x = x_ref[...]
    h = (_rms(x) * g2_ref[...] * (1.0 + sc_ref[0]) + sh_ref[0]).astype(BF16)
    blk = D_FF // FF_SPLIT
    acc = None
    for c in range(FF_SPLIT):
        a = jnp.maximum(_dot(h, wup_ref[:, c * blk:(c + 1) * blk]), 0.0)
        part = _dot((a * a).astype(BF16), wdn_ref[c * blk:(c + 1) * blk, :])
        acc = part if acc is None else acc + part
    out = x + gt_ref[0] * acc
    o_ref[...] = out
    y_ref[...] = _rms(out) * gfin_ref[...]


def _mlp_call(x, ada, g2, wup, wdn, gfin, *, tm, tiles_per_batch):
    t = x.shape[0]
    row = lambda n: pl.BlockSpec((tm, n), lambda i: (i, 0))
    return pl.pallas_call(
        _mlp_kernel,
        grid=(t // tm,),
        in_specs=[row(D_MODEL), _mod_spec(ada, tiles_per_batch, 3), _mod_spec(ada, tiles_per_batch, 4),
                  _mod_spec(ada, tiles_per_batch, 5), _full(g2), _full(wup), _full(wdn), _full(gfin)],
        out_specs=[row(D_MODEL), row(D_MODEL)],
        out_shape=[jax.ShapeDtypeStruct(x.shape, F32), jax.ShapeDtypeStruct(x.shape, F32)],
        compiler_params=_cparams(("arbitrary",)),
        name="mlp",
    )(x, ada, ada, ada, g2, wup, wdn, gfin)


def _mix_weights(w_s_l, b_s_l, seq, rows):
    cl = min(seq, CHUNK)
    w = jnp.tril(w_s_l[:, :cl, :cl])
    reps = rows // cl
    eye = jnp.eye(reps, dtype=w.dtype)
    wmix = jnp.einsum("gts,ab->gatbs", w, eye).reshape(GMLP_GROUPS, rows, rows).astype(BF16)
    bias = jnp.tile(jnp.transpose(b_s_l[:, :cl]), (reps, 1))
    bmix = jnp.repeat(bias, GMLP_GROUP_CH, axis=1)
    return wmix, bmix


def _cumsum_tri(rows, seq):
    r = jnp.arange(rows)
    return ((r[:, None] >= r[None, :]) & (r[:, None] // seq == r[None, :] // seq)).astype(BF16)


def kernel(x_prompt, x_sample, c_prompt, c_sample, cache_k, cache_v, cache_logf, state_ssm_re, state_ssm_im, page_table, w_ada, b_ada, g_norm1, w_in, b_f, ssm_a_re, ssm_a_im, ssm_log_dt, ssm_b_re, ssm_b_im, ssm_c_re, ssm_c_im, ssm_d, w_glu, g_gv, w_s, b_s, w_gmlp_out, w_att_out, w_o, g_norm2, w_up, w_down, g_final):
    depth = w_in.shape[0]
    bp, lp, d = x_prompt.shape
    bs, ls, _ = x_sample.shape
    tp, ts = bp * lp, bs * ls
    tm_p = TM_PROMPT
    tiles_pb = lp // tm_p

    ada = _ada_call(jnp.concatenate([c_prompt, c_sample], axis=0), w_ada, b_ada)
    n_pool, page = cache_k.shape[1], cache_k.shape[2]
    feature_major = lambda c: jnp.transpose(c, (0, 1, 3, 4, 2)).reshape(depth, n_pool, ATT_WIDTH, page)
    cache_kt = feature_major(cache_k)
    cache_vt = feature_major(cache_v)
    cache_lf_t = jnp.swapaxes(cache_logf, 2, 3)

    xp = x_prompt.reshape(tp, d)
    xs = x_sample.reshape(ts, d)
    tri_p = _cumsum_tri(tm_p, tm_p)
    tri_s = _cumsum_tri(ts, ls)
    row2 = lambda a: a.reshape(1, -1)
    outs = {k: [] for k in ("kp", "vp", "lfp", "hrp", "hip", "gvp", "ks", "vs", "lfs", "hrs", "his", "gvs")}
    yp = ys = None
    chunk_start = ((lp - 1) // CHUNK) * CHUNK
    pad_f = lambda w: jnp.pad(w, ((0, 0), (0, LANE - ATT_HEADS)))
    for l in range(depth):
        wi = w_in[l]
        w_qkv, w_f, w_sgg = wi[:, Q_OFF:F_OFF], wi[:, F_OFF:S_OFF], wi[:, S_OFF:G_OFF]
        w_row_p = jnp.concatenate([wi[:, K_OFF:V_OFF], w_sgg, pad_f(w_f)], axis=1).astype(BF16)
        w_t_p = jnp.pad(jnp.concatenate([w_qkv, w_f], axis=1).T, ((0, BF16_ROWS - ATT_HEADS), (0, 0))).astype(BF16)
        w_row_s = jnp.concatenate([w_qkv, w_sgg, pad_f(w_f)], axis=1).astype(BF16)
        w_gate = wi[:, G_OFF:].astype(BF16)
        bf_row = jnp.pad(b_f[l], (0, LANE - ATT_HEADS)).reshape(1, LANE)
        bf_col = b_f[l].reshape(ATT_HEADS, 1)
        g1, g2, ggv, dskip = row2(g_norm1[l]), row2(g_norm2[l]), row2(g_gv[l]), row2(ssm_d[l])
        wglu, wgm, wat, wo = (w_glu[l].astype(BF16), w_gmlp_out[l].astype(BF16), w_att_out[l].astype(BF16),
                              w_o[l].astype(BF16))
        wup, wdn = w_up[l].astype(BF16), w_down[l].astype(BF16)
        gfin = row2(g_final)
        s5w = _s5_weights(ssm_a_re[l], ssm_a_im[l], ssm_log_dt[l], ssm_b_re[l], ssm_b_im[l], ssm_c_re[l],
                          ssm_c_im[l], lp // S5_CHUNK)

        ada_p = ada[l, :bp].reshape(bp, 1, 6 * d)
        wmix, bmix = _mix_weights(w_s[l], b_s[l], lp, CHUNK)
        qt, ktf, vtf, vtb, kb, lft, cumt, fkb, u_lo, u_hi, gated, gv = _in_proj_p_call(
            xp, ada_p, g1, w_row_p, w_t_p, bf_row, bf_col, ggv, wmix, bmix, tri_p, tri_p.T, bsz=bp, seq=lp, tm=tm_p)
        att_t = _flash_call(kb, qt, vtb, fkb, cumt, bsz=bp, seq=lp, tq=tm_p)
        ys_lo, ys_hi, hr, hi = _s5_seq_call(u_lo, u_hi, dskip, s5w, bsz=bp, seq=lp)
        xp = _merge_call(xp, ada_p, g1, w_gate, ys_lo, ys_hi, wglu, gated, wgm, att_t, wat, wo,
                         tm=tm_p, tiles_per_batch=tiles_pb)
        xp, yp = _mlp_call(xp, ada_p, g2, wup, wdn, gfin, tm=tm_p, tiles_per_batch=tiles_pb)
        token_major = lambda a: jnp.transpose(a.reshape(bp, ATT_HEADS, HEAD_DIM, lp), (0, 3, 1, 2))
        outs["kp"].append(token_major(ktf))
        outs["vp"].append(token_major(vtf))
        outs["lfp"].append(jnp.swapaxes(lft, 1, 2))
        outs["hrp"].append(hr)
        outs["hip"].append(hi)
        outs["gvp"].append(gv.reshape(bp, lp, GMLP_WIDTH)[:, chunk_start:])

        ada_s = jnp.repeat(ada[l, bp:], ls, axis=0).reshape(1, ts, 6 * d)
        wmix, bmix = _mix_weights(w_s[l], b_s[l], ls, ts)
        q, kf, kb, vf, vb, lf, cum, u, gated, gv = _in_proj_s_call(xs, ada_s, g1, w_row_s, bf_row, ggv, wmix, bmix, tri_s)
        cum_t = jnp.swapaxes(cum.reshape(bs, ls, ATT_HEADS), 1, 2)
        att = _decode_call(page_table, q, kb, vb, cum, cum_t, cache_kt, cache_vt, cache_lf_t,
                           layer=l, bsz=bs, n_new=ls)
        ys_lo, ys_hi, hr, hi = _s5_step_call(u, dskip, state_ssm_re[l], state_ssm_im[l], s5w, bsz=bs, n_new=ls)
        xs = _merge_call(xs, ada_s, g1, w_gate, ys_lo, ys_hi, wglu, gated, wgm, att, wat, wo,
                         tm=ts, tiles_per_batch=1)
        xs, ys = _mlp_call(xs, ada_s, g2, wup, wdn, gfin, tm=ts, tiles_per_batch=1)
        outs["ks"].append(kf.reshape(bs, ls, ATT_HEADS, HEAD_DIM))
        outs["vs"].append(vf.reshape(bs, ls, ATT_HEADS, HEAD_DIM))
        outs["lfs"].append(lf.reshape(bs, ls, ATT_HEADS))
        outs["hrs"].append(hr)
        outs["his"].append(hi)
        outs["gvs"].append(gv.reshape(bs, ls, GMLP_WIDTH))

    st = lambda k: jnp.stack(outs[k])
    return (yp.reshape(bp, lp, d), ys.reshape(bs, ls, d), st("kp"), st("vp"), st("lfp"), st("hrp"), st("hip"), st("gvp"),
            st("ks"), st("vs"), st("lfs"), st("hrs"), st("his"), st("gvs"))
```

```python
import functools
import math

import jax
import jax.numpy as jnp
from jax import lax
from jax.experimental import pallas as pl
from jax.experimental.pallas import tpu as pltpu

F32 = jnp.float32
BF16 = jnp.bfloat16

D_MODEL = 1024
ATT_HEADS = 8
HEAD_DIM = 64
ATT_WIDTH = ATT_HEADS * HEAD_DIM
SSM_GROUPS = 16
SSM_GROUP_CH = 16
SSM_WIDTH = SSM_GROUPS * SSM_GROUP_CH
SSM_STATE = 64
N_STATE = SSM_GROUPS * SSM_STATE
GMLP_GROUPS = 4
GMLP_GROUP_CH = 64
GMLP_WIDTH = GMLP_GROUPS * GMLP_GROUP_CH
CHUNK = 128
N_BRANCH = 3
D_FF = 4 * D_MODEL
EPS = 1e-6

Q_OFF = 0
K_OFF = Q_OFF + ATT_WIDTH
V_OFF = K_OFF + ATT_WIDTH
F_OFF = V_OFF + ATT_WIDTH
S_OFF = F_OFF + ATT_HEADS
GU_OFF = S_OFF + SSM_WIDTH
GV_OFF = GU_OFF + GMLP_WIDTH
G_OFF = GV_OFF + GMLP_WIDTH

LOG2E = math.log2(math.e)
LANE = 128
BF16_ROWS = 16
S5_CHUNK = 16
TM_PROMPT = 512
FLASH_HEADS = 8
DEC_PAGES = 8
DEC_SLOTS = 4
FF_SPLIT = 4
VMEM_LIMIT = 56 * 1024 * 1024


def _cparams(sem):
    return pltpu.CompilerParams(dimension_semantics=sem, vmem_limit_bytes=VMEM_LIMIT)


def _dot(a, b):
    return jnp.dot(a, b, preferred_element_type=F32)


def _dot_nt(a, b):
    return lax.dot_general(a, b, (((1,), (1,)), ((), ())), preferred_element_type=F32)


def _dot_tn(a, b):
    return lax.dot_general(a, b, (((0,), (0,)), ((), ())), preferred_element_type=F32)


def _split3(x):
    hi = x.astype(BF16)
    r1 = x - hi.astype(F32)
    mid = r1.astype(BF16)
    lo = (r1 - mid.astype(F32)).astype(BF16)
    return hi, mid, lo


def _gelu(x):
    return 0.5 * x * (1.0 + jnp.tanh(math.sqrt(2.0 / math.pi) * (x + 0.044715 * (x * x * x))))


def _log_sigmoid(x):
    return -(jnp.maximum(-x, 0.0) + jnp.log1p(jnp.exp(-jnp.abs(x))))


def _rms(x):
    return x * lax.rsqrt(jnp.mean(x * x, axis=-1, keepdims=True) + EPS)


class _Layer:
    def __init__(self, stacked, index):
        self.stacked, self.index = stacked, index

    @property
    def shape(self):
        return self.stacked.shape[1:]

    @property
    def ndim(self):
        return self.stacked.ndim - 1


def _raw(args):
    return [a.stacked if isinstance(a, _Layer) else a for a in args]


def _full(arr):
    if isinstance(arr, _Layer):
        l, shape, zeros = arr.index, arr.shape, (0,) * arr.ndim
        return pl.BlockSpec((None,) + shape, lambda *_: (l,) + zeros)
    return pl.BlockSpec(arr.shape, lambda *_: (0,) * arr.ndim)


def _ada_kernel(c_ref, w_ref, b_ref, o_ref):
    c = c_ref[...]
    a = (c * jax.nn.sigmoid(c)).astype(BF16)
    o_ref[0] = _dot(a, w_ref[0].astype(BF16)) + b_ref[0]


def _ada_call(c_all, w_ada, b_ada):
    depth, d, n = w_ada.shape
    r = c_all.shape[0]
    tn = 1536
    return pl.pallas_call(
        _ada_kernel,
        grid=(depth, n // tn),
        in_specs=[pl.BlockSpec((r, d), lambda l, j: (0, 0)),
                  pl.BlockSpec((1, d, tn), lambda l, j: (l, 0, j)),
                  pl.BlockSpec((1, 1, tn), lambda l, j: (l, 0, j))],
        out_specs=pl.BlockSpec((1, r, tn), lambda l, j: (l, 0, j)),
        out_shape=jax.ShapeDtypeStruct((depth, r, n), F32),
        compiler_params=_cparams(("arbitrary", "arbitrary")),
        name="ada",
    )(c_all, w_ada, b_ada.reshape(depth, 1, n))


def _mod_spec(ada, tiles_per_batch, j):
    l, r = ada.index, ada.shape[1]
    return pl.BlockSpec((None, 1, r, D_MODEL), lambda i: (l, i // tiles_per_batch, 0, j))


def _gmlp_gate(gu, gv, wmix_ref, bmix_ref, gated_ref, mix_rows):
    lane = lax.broadcasted_iota(jnp.int32, (1, GMLP_WIDTH), 1)
    for c in range(gu.shape[0] // mix_rows):
        rows = slice(c * mix_rows, (c + 1) * mix_rows)
        gvc = gv[rows]
        s = bmix_ref[...]
        for g in range(GMLP_GROUPS):
            in_g = (lane >= g * GMLP_GROUP_CH) & (lane < (g + 1) * GMLP_GROUP_CH)
            s = s + _dot(wmix_ref[g], jnp.where(in_g, gvc, 0.0).astype(BF16))
        gated_ref[rows, :] = (gu[rows] * s).astype(BF16)


W_ROW_COLS = ATT_WIDTH + SSM_WIDTH + 2 * GMLP_WIDTH + LANE
W_T_ROWS = 3 * ATT_WIDTH + BF16_ROWS


def _in_proj_p_kernel(x_ref, sh_ref, sc_ref, g1_ref, wr_ref, wt_ref, bfr_ref, bfc_ref, ggv_ref, wmix_ref, bmix_ref,
                      tri_ref, triu_ref,
                      qt_ref, ktf_ref, vtf_ref, vtb_ref, kb_ref, lft_ref, cumt_ref, fkb_ref, ulo_ref, uhi_ref,
                      gated_ref, gv_ref, carry_r, carry_c, *, tiles_per_batch, mix_rows):
    i = pl.program_id(0)
    tm = x_ref.shape[0]
    a = ATT_WIDTH
    h = (_rms(x_ref[...]) * g1_ref[...] * (1.0 + sc_ref[0]) + sh_ref[0]).astype(BF16)

    @pl.when(i % tiles_per_batch == 0)
    def _():
        carry_r[...] = jnp.zeros_like(carry_r)
        carry_c[...] = jnp.zeros_like(carry_c)

    zt = _dot_nt(wt_ref[...], h)
    qt_ref[0] = (zt[0:a] * (LOG2E * HEAD_DIM ** -0.5)).astype(BF16)
    ktf_ref[0] = zt[a:2 * a]
    vt = zt[2 * a:3 * a]
    vtf_ref[0] = vt
    vtb_ref[0] = vt.astype(BF16)
    lft = _log_sigmoid(zt[3 * a:3 * a + ATT_HEADS] + bfc_ref[...])
    lft_ref[0] = lft
    hi, mid, lo = _split3(lft)
    triu = triu_ref[...]
    cumt = (_dot(hi, triu) + _dot(mid, triu)) + _dot(lo, triu) + carry_c[...]
    cumt_ref[0] = cumt * LOG2E
    carry_c[...] = cumt[:, tm - 1:tm]

    kb_ref[...] = _dot(h, wr_ref[:, 0:a]).astype(BF16)
    o = a
    u = _dot(h, wr_ref[:, o:o + SSM_WIDTH])
    ulo_ref[...] = u[:, :LANE]
    uhi_ref[...] = u[:, LANE:]
    o += SSM_WIDTH
    gu = _gelu(_dot(h, wr_ref[:, o:o + GMLP_WIDTH]))
    o += GMLP_WIDTH
    gv = _rms(_gelu(_dot(h, wr_ref[:, o:o + GMLP_WIDTH]))) * ggv_ref[...]
    gv_ref[...] = gv
    o += GMLP_WIDTH

    lf = _log_sigmoid(_dot(h, wr_ref[:, o:o + LANE]) + bfr_ref[...])
    hi, mid, lo = _split3(lf)
    tri = tri_ref[...]
    cum = (_dot(tri, hi) + _dot(tri, mid)) + _dot(tri, lo) + carry_r[...]
    carry_r[...] = cum[tm - 1:tm, :]
    for hh in range(ATT_HEADS):
        fkb_ref[0, hh] = jnp.broadcast_to(cum[:, hh:hh + 1] * LOG2E, (tm, LANE))

    _gmlp_gate(gu, gv, wmix_ref, bmix_ref, gated_ref, mix_rows)


def _in_proj_p_call(x, ada, g1, wr, wt, bfr, bfc, ggv, wmix, bmix, tri, triu, *, bsz, seq, tm):
    t = x.shape[0]
    tpb = seq // tm
    row = lambda n: pl.BlockSpec((tm, n), lambda i: (i, 0))
    fm = lambda n: pl.BlockSpec((1, n, tm), lambda i: (i // tpb, 0, i % tpb))
    outs = [(fm(ATT_WIDTH), (bsz, ATT_WIDTH, seq), BF16),
            (fm(ATT_WIDTH), (bsz, ATT_WIDTH, seq), F32),
            (fm(ATT_WIDTH), (bsz, ATT_WIDTH, seq), F32),
            (fm(ATT_WIDTH), (bsz, ATT_WIDTH, seq), BF16),
            (row(ATT_WIDTH), (t, ATT_WIDTH), BF16),
            (fm(ATT_HEADS), (bsz, ATT_HEADS, seq), F32),
            (fm(ATT_HEADS), (bsz, ATT_HEADS, seq), F32),
            (pl.BlockSpec((1, ATT_HEADS, tm, LANE), lambda i: (i // tpb, 0, i % tpb, 0)),
             (bsz, ATT_HEADS, seq, LANE), F32),
            (row(LANE), (t, LANE), F32),
            (row(LANE), (t, LANE), F32),
            (row(GMLP_WIDTH), (t, GMLP_WIDTH), BF16),
            (row(GMLP_WIDTH), (t, GMLP_WIDTH), F32)]
    return pl.pallas_call(
        functools.partial(_in_proj_p_kernel, tiles_per_batch=tpb, mix_rows=wmix.shape[1]),
        grid=(t // tm,),
        in_specs=[row(D_MODEL), _mod_spec(ada, tpb, 0), _mod_spec(ada, tpb, 1), _full(g1), _full(wr), _full(wt),
                  _full(bfr), _full(bfc), _full(ggv), _full(wmix), _full(bmix), _full(tri), _full(triu)],
        out_specs=[o[0] for o in outs],
        out_shape=[jax.ShapeDtypeStruct(o[1], o[2]) for o in outs],
        scratch_shapes=[pltpu.VMEM((1, LANE), F32), pltpu.VMEM((ATT_HEADS, 1), F32)],
        compiler_params=_cparams(("arbitrary",)),
        name="in_proj_p",
    )(*_raw([x, ada, ada, g1, wr, wt, bfr, bfc, ggv, wmix, bmix, tri, triu]))


W_S_COLS = 3 * ATT_WIDTH + SSM_WIDTH + 2 * GMLP_WIDTH + LANE


def _in_proj_s_kernel(x_ref, sh_ref, sc_ref, g1_ref, w_ref, bf_ref, ggv_ref, wmix_ref, bmix_ref, tri_ref,
                      q_ref, kf_ref, kb_ref, vf_ref, vb_ref, lf_ref, cum_ref, u_ref, gated_ref, gv_ref, *, mix_rows):
    h = (_rms(x_ref[...]) * g1_ref[...] * (1.0 + sc_ref[0]) + sh_ref[0]).astype(BF16)
    a = ATT_WIDTH
    q_ref[...] = (_dot(h, w_ref[:, 0:a]) * (HEAD_DIM ** -0.5)).astype(BF16)
    k = _dot(h, w_ref[:, a:2 * a])
    kf_ref[...] = k
    kb_ref[...] = k.astype(BF16)
    v = _dot(h, w_ref[:, 2 * a:3 * a])
    vf_ref[...] = v
    vb_ref[...] = v.astype(BF16)
    o = 3 * a
    u_ref[...] = _dot(h, w_ref[:, o:o + SSM_WIDTH])
    o += SSM_WIDTH
    gu = _gelu(_dot(h, w_ref[:, o:o + GMLP_WIDTH]))
    o += GMLP_WIDTH
    gv = _rms(_gelu(_dot(h, w_ref[:, o:o + GMLP_WIDTH]))) * ggv_ref[...]
    gv_ref[...] = gv
    o += GMLP_WIDTH
    lf = _log_sigmoid(_dot(h, w_ref[:, o:o + LANE]) + bf_ref[...])
    lf_ref[...] = lf[:, :ATT_HEADS]
    hi, mid, lo = _split3(lf)
    tri = tri_ref[...]
    cum = (_dot(tri, hi) + _dot(tri, mid)) + _dot(tri, lo)
    cum_ref[...] = cum[:, :ATT_HEADS]
    _gmlp_gate(gu, gv, wmix_ref, bmix_ref, gated_ref, mix_rows)


def _in_proj_s_call(x, ada, g1, w, bf, ggv, wmix, bmix, tri):
    t = x.shape[0]
    row = lambda n: pl.BlockSpec((t, n), lambda i: (0, 0))
    outs = [(ATT_WIDTH, BF16), (ATT_WIDTH, F32), (ATT_WIDTH, BF16), (ATT_WIDTH, F32), (ATT_WIDTH, BF16),
            (ATT_HEADS, F32), (ATT_HEADS, F32), (SSM_WIDTH, F32), (GMLP_WIDTH, BF16), (GMLP_WIDTH, F32)]
    return pl.pallas_call(
        functools.partial(_in_proj_s_kernel, mix_rows=wmix.shape[1]),
        grid=(1,),
        in_specs=[row(D_MODEL), _mod_spec(ada, 1, 0), _mod_spec(ada, 1, 1),
                  _full(g1), _full(w), _full(bf), _full(ggv), _full(wmix), _full(bmix), _full(tri)],
        out_specs=[row(n) for n, _ in outs],
        out_shape=[jax.ShapeDtypeStruct((t, n), dt) for n, dt in outs],
        compiler_params=_cparams(("arbitrary",)),
        name="in_proj_s",
    )(*_raw([x, ada, ada, g1, w, bf, ggv, wmix, bmix, tri]))


def _flash_kernel(qi_tab, ki_tab, k_ref, qt_ref, vt_ref, fkb_ref, fq_ref, o_ref, m_sc, l_sc, acc_sc):
    hg = pl.program_id(1)
    s_idx = pl.program_id(2)
    qi = qi_tab[s_idx]
    ki = ki_tab[s_idx]
    tk = k_ref.shape[0]
    tq = qt_ref.shape[2]
    pair = 2 * HEAD_DIM

    @pl.when(ki == 0)
    def _():
        m_sc[...] = jnp.full_like(m_sc, -jnp.inf)
        l_sc[...] = jnp.zeros_like(l_sc)
        acc_sc[...] = jnp.zeros_like(acc_sc)

    def step(masked):
        feat = lax.broadcasted_iota(jnp.int32, (pair, 1), 0)
        sts = []
        for h in range(FLASH_HEADS):
            hp, hh = divmod(h, 2)
            k = k_ref[:, hp * pair:(hp + 1) * pair]
            qt = qt_ref[0, hp * pair:(hp + 1) * pair, :]
            own = (feat >= hh * HEAD_DIM) & (feat < (hh + 1) * HEAD_DIM)
            st = _dot(k, jnp.where(own, qt, jnp.zeros_like(qt)))
            st = st - jnp.concatenate([fkb_ref[0, h]] * (tq // LANE), axis=1)
            if masked:
                kj = lax.broadcasted_iota(jnp.int32, (tk, tq), 0)
                qc = lax.broadcasted_iota(jnp.int32, (tk, tq), 1)
                st = jnp.where(kj <= qc, st, -jnp.inf)
            sts.append(st)
        for h in range(FLASH_HEADS):
            st = sts[h]
            fq = fq_ref[0, pl.ds(hg * FLASH_HEADS + h, 1), :]
            m_prev = m_sc[h]
            m_new = jnp.maximum(m_prev, fq + jnp.max(st, axis=0, keepdims=True))
            alpha = jnp.exp2(m_prev - m_new)
            p = jnp.exp2(st + (fq - m_new))
            l_sc[h] = alpha * l_sc[h] + jnp.sum(p, axis=0, keepdims=True)
            acc_sc[h] = alpha * acc_sc[h] + _dot(vt_ref[0, h * HEAD_DIM:(h + 1) * HEAD_DIM, :], p.astype(BF16))
            m_sc[h] = m_new

    @pl.when(ki < qi)
    def _():
        step(False)

    @pl.when(ki == qi)
    def _():
        step(True)
        for h in range(FLASH_HEADS):
            o_ref[0, h * HEAD_DIM:(h + 1) * HEAD_DIM, :] = (acc_sc[h] / l_sc[h]).astype(o_ref.dtype)


def _flash_call(k, qt, vt, fkb, cumt, *, bsz, seq, tq):
    nq = seq // tq
    tri = [(a, b) for a in range(nq) for b in range(a + 1)]
    qi_tab = jnp.array([a for a, _ in tri], jnp.int32)
    ki_tab = jnp.array([b for _, b in tri], jnp.int32)
    width = FLASH_HEADS * HEAD_DIM
    q_spec = pl.BlockSpec((1, width, tq), lambda b, hg, s, qt_, kt_: (b, hg, qt_[s]))
    return pl.pallas_call(
        _flash_kernel,
        grid_spec=pltpu.PrefetchScalarGridSpec(
            num_scalar_prefetch=2,
            grid=(bsz, ATT_HEADS // FLASH_HEADS, len(tri)),
            in_specs=[pl.BlockSpec((tq, width), lambda b, hg, s, qt_, kt_: (b * nq + kt_[s], hg)),
                      q_spec,
                      pl.BlockSpec((1, width, tq), lambda b, hg, s, qt_, kt_: (b, hg, kt_[s])),
                      pl.BlockSpec((1, FLASH_HEADS, tq, LANE), lambda b, hg, s, qt_, kt_: (b, hg, kt_[s], 0)),
                      pl.BlockSpec((1, ATT_HEADS, tq), lambda b, hg, s, qt_, kt_: (b, 0, qt_[s]))],
            out_specs=q_spec,
            scratch_shapes=[pltpu.VMEM((FLASH_HEADS, 1, tq), F32), pltpu.VMEM((FLASH_HEADS, 1, tq), F32),
                            pltpu.VMEM((FLASH_HEADS, HEAD_DIM, tq), F32)]),
        out_shape=jax.ShapeDtypeStruct(qt.shape, BF16),
        compiler_params=_cparams(("arbitrary", "arbitrary", "arbitrary")),
        name="flash",
    )(qi_tab, ki_tab, k, qt, vt, fkb, cumt)


def _decode_kernel(pt_ref, q_ref, kn_ref, vn_ref, cn_ref, cnt_ref, hmask_ref, qmask_ref, nmask_ref, tri_ref,
                   ck_hbm, cv_hbm, clf_hbm, o_ref, kbuf, vbuf, lbuf, sem, *, layer, n_pages):
    b = pl.program_id(0)
    nb = pl.num_programs(0)
    page = lbuf.shape[3]
    n_chunks = n_pages // DEC_PAGES
    n_q = q_ref.shape[0]
    rows = n_q * ATT_HEADS

    def copies(bb, chunk, slot):
        out = []
        for j in range(DEC_PAGES):
            pg = pt_ref[bb, chunk * DEC_PAGES + j]
            cols = pl.ds(j * page, page)
            out.append(pltpu.make_async_copy(ck_hbm.at[layer, pg], kbuf.at[slot, :, cols], sem.at[slot, 0]))
            out.append(pltpu.make_async_copy(cv_hbm.at[layer, pg], vbuf.at[slot, :, cols], sem.at[slot, 1]))
            out.append(pltpu.make_async_copy(clf_hbm.at[layer, pg], lbuf.at[slot, j], sem.at[slot, 2]))
        return out

    def chunk_of(step):
        return n_chunks - 1 - step

    ahead = DEC_SLOTS - 1

    @pl.when(b == 0)
    def _():
        for step in range(ahead):
            for cp in copies(b, chunk_of(step), step % DEC_SLOTS):
                cp.start()

    q = q_ref[...].astype(F32)
    qbd = jnp.broadcast_to(q[:, None, :], (n_q, ATT_HEADS, ATT_WIDTH)).reshape(rows, ATT_WIDTH)
    qbd = jnp.where(hmask_ref[...] > 0, qbd, 0.0).astype(BF16)
    cn = cn_ref[...]
    cn_rows = jnp.broadcast_to(cn[:, None, :], (n_q, ATT_HEADS, ATT_HEADS)).reshape(rows, ATT_HEADS)
    fq = jnp.sum(jnp.where(qmask_ref[...] > 0, cn_rows, 0.0), axis=-1, keepdims=True)

    s = _dot_nt(qbd, kn_ref[...])
    fk_new = jnp.concatenate([cnt_ref[0]] * n_q, axis=0)
    s = jnp.where(nmask_ref[...] > 0, s + fq - fk_new, -jnp.inf)
    m = jnp.max(s, axis=-1, keepdims=True)
    p = jnp.exp(s - m)
    l = jnp.sum(p, axis=-1, keepdims=True)
    acc = _dot(p.astype(BF16), vn_ref[...])
    run = jnp.zeros((ATT_HEADS, 1), F32)

    for step in range(n_chunks):
        slot = step % DEC_SLOTS
        for cp in copies(b, chunk_of(step), slot):
            cp.wait()
        nxt = step + ahead
        if nxt < n_chunks:
            for cp in copies(b, chunk_of(nxt), nxt % DEC_SLOTS):
                cp.start()
        else:
            @pl.when(b + 1 < nb)
            def _(nxt=nxt):
                for cp in copies(b + 1, chunk_of(nxt - n_chunks), nxt % DEC_SLOTS):
                    cp.start()

        kc = kbuf[slot].astype(BF16)
        vc = vbuf[slot].astype(BF16)
        lf = lbuf[slot].reshape(DEC_PAGES * ATT_HEADS, page)
        hi, mid, lo = _split3(lf)
        tri = tri_ref[...]
        suf = (_dot(hi, tri) + _dot(mid, tri)) + _dot(lo, tri)
        tot = jnp.sum(lf, axis=-1, keepdims=True)
        bias = []
        for j in reversed(range(DEC_PAGES)):
            pr = slice(j * ATT_HEADS, (j + 1) * ATT_HEADS)
            bias.append(suf[pr] + run)
            run = run + tot[pr]
        bias = jnp.concatenate(bias[::-1], axis=1)
        s = _dot(qbd, kc) + fq + jnp.concatenate([bias] * n_q, axis=0)
        m_new = jnp.maximum(m, jnp.max(s, axis=-1, keepdims=True))
        alpha = jnp.exp(m - m_new)
        p = jnp.exp(s - m_new)
        l = alpha * l + jnp.sum(p, axis=-1, keepdims=True)
        acc = alpha * acc + _dot_nt(p.astype(BF16), vc)
        m = m_new

    out = jnp.where(hmask_ref[...] > 0, acc / l, 0.0)
    o_ref[...] = jnp.sum(out.reshape(n_q, ATT_HEADS, ATT_WIDTH), axis=1).astype(o_ref.dtype)


def _decode_call(page_table, q, k_new, v_new, cn, cn_t, cache_kt, cache_vt, cache_lf_t, *, layer, bsz, n_new):
    n_pages = page_table.shape[1]
    assert n_pages % (DEC_PAGES * DEC_SLOTS) == 0
    page = cache_kt.shape[3]
    rows = n_new * ATT_HEADS
    r = jnp.arange(rows)
    hmask = (r[:, None] % ATT_HEADS == jnp.arange(ATT_WIDTH)[None, :] // HEAD_DIM).astype(F32)
    qmask = (r[:, None] % ATT_HEADS == jnp.arange(ATT_HEADS)[None, :]).astype(F32)
    nmask = (r[:, None] // ATT_HEADS >= jnp.arange(n_new)[None, :]).astype(F32)
    tri = (jnp.arange(page)[:, None] > jnp.arange(page)[None, :]).astype(BF16)
    tok = lambda n: pl.BlockSpec((n_new, n), lambda b, pt: (b, 0))
    hbm = pl.BlockSpec(memory_space=pl.ANY)
    return pl.pallas_call(
        functools.partial(_decode_kernel, layer=layer, n_pages=n_pages),
        grid_spec=pltpu.PrefetchScalarGridSpec(
            num_scalar_prefetch=1,
            grid=(bsz,),
            in_specs=[tok(ATT_WIDTH), tok(ATT_WIDTH), tok(ATT_WIDTH), tok(ATT_HEADS),
                      pl.BlockSpec((1, ATT_HEADS, n_new), lambda b, pt: (b, 0, 0)),
                      _full(hmask), _full(qmask), _full(nmask), _full(tri), hbm, hbm, hbm],
            out_specs=tok(ATT_WIDTH),
            scratch_shapes=[pltpu.VMEM((DEC_SLOTS, ATT_WIDTH, DEC_PAGES * page), F32),
                            pltpu.VMEM((DEC_SLOTS, ATT_WIDTH, DEC_PAGES * page), F32),
                            pltpu.VMEM((DEC_SLOTS, DEC_PAGES, ATT_HEADS, page), F32),
                            pltpu.SemaphoreType.DMA((DEC_SLOTS, 3))]),
        out_shape=jax.ShapeDtypeStruct(q.shape, BF16),
        compiler_params=_cparams(("arbitrary",)),
        name="decode",
    )(page_table, q, k_new, v_new, cn, cn_t, hmask, qmask, nmask, tri, cache_kt, cache_vt, cache_lf_t)


def _s5_discretise(a_re, a_im, log_dt, b_re, b_im):
    dt = jnp.exp(log_dt)[:, None]
    rate = a_re * dt
    ang = a_im * dt
    mag = jnp.exp(rate)
    ab_re = mag * jnp.cos(ang)
    ab_im = mag * jnp.sin(ang)
    den = a_re * a_re + a_im * a_im
    z_re = ((ab_re - 1.0) * a_re + ab_im * a_im) / den
    z_im = (ab_im * a_re - (ab_re - 1.0) * a_im) / den
    bb_re = z_re[..., None] * b_re - z_im[..., None] * b_im
    bb_im = z_re[..., None] * b_im + z_im[..., None] * b_re
    return rate, ang, ab_re, ab_im, bb_re, bb_im


def _abar_pow(rate, ang, j):
    mag = jnp.exp(rate * j)
    return mag * jnp.cos(ang * j), mag * jnp.sin(ang * j)


def _group_diag(m):
    g, r, c = m.shape
    return jnp.einsum("grc,gh->grhc", m, jnp.eye(g, dtype=m.dtype)).reshape(g * r, g * c)


def _s5_weights(a_re, a_im, log_dt, b_re, b_im, c_re, c_im, chunks_per_seq):
    hp = lax.Precision.HIGHEST
    rate, ang, ab_re, ab_im, bb_re, bb_im = _s5_discretise(a_re, a_im, log_dt, b_re, b_im)
    c = S5_CHUNK
    bfull = jnp.concatenate([_group_diag(jnp.swapaxes(bb_re, 1, 2)), _group_diag(jnp.swapaxes(bb_im, 1, 2))], axis=1)
    cfull = jnp.concatenate([_group_diag(jnp.swapaxes(c_re, 1, 2)), -_group_diag(jnp.swapaxes(c_im, 1, 2))], axis=0)
    j = jnp.arange(c, dtype=F32)[:, None, None]
    p_re, p_im = _abar_pow(rate[None], ang[None], j)
    cb_re = c_re[None] * p_re[:, :, None, :] - c_im[None] * p_im[:, :, None, :]
    cb_im = c_re[None] * p_im[:, :, None, :] + c_im[None] * p_re[:, :, None, :]
    kern = jnp.einsum("jgmp,gpn->jgnm", cb_re, bb_re, precision=hp) - jnp.einsum("jgmp,gpn->jgnm", cb_im, bb_im, precision=hp)
    kst = jnp.concatenate([_group_diag(kern[c - 1 - s]) for s in range(c)], axis=0)
    n_steps = max(1, (chunks_per_seq - 1).bit_length())
    d = (c * 2.0 ** jnp.arange(n_steps, dtype=F32))[:, None, None]
    s_re, s_im = _abar_pow(rate[None], ang[None], d)
    flat = lambda a: a.reshape(a.shape[0], 1, N_STATE)
    return dict(a_re=ab_re.reshape(1, N_STATE), a_im=ab_im.reshape(1, N_STATE), b=bfull.astype(BF16),
                c=cfull.astype(BF16), kst=kst.astype(BF16), s_re=flat(s_re), s_im=flat(s_im))


def _s5_seq_kernel(ulo_ref, uhi_ref, d_ref, b_ref, c_ref, kst_ref, are_ref, aim_ref, sre_ref, sim_ref,
                   ylo_ref, yhi_ref, hre_ref, him_ref, u2_sc):
    w = SSM_WIDTH
    n = ulo_ref.shape[0] // S5_CHUNK
    a_re = are_ref[...]
    a_im = aim_ref[...]
    token = lambda ref, s: ref[pl.ds(s, n, stride=S5_CHUNK), :]
    h_re = h_im = None
    for s in range(S5_CHUNK):
        ub = jnp.concatenate([token(ulo_ref, s), token(uhi_ref, s)], axis=1).astype(BF16)
        u2_sc[:, s * w:(s + 1) * w] = ub
        x = _dot(ub, b_ref[...])
        x_re, x_im = x[:, :N_STATE], x[:, N_STATE:]
        if s == 0:
            h_re, h_im = x_re, x_im
        else:
            h_re, h_im = a_re * h_re - a_im * h_im + x_re, a_re * h_im + a_im * h_re + x_im
    pos = lax.broadcasted_iota(jnp.int32, (n, 1), 0)
    for k in range(sre_ref.shape[0]):
        d = 1 << k
        keep = pos >= d
        p_re = jnp.where(keep, pltpu.roll(h_re, d, 0), 0.0)
        p_im = jnp.where(keep, pltpu.roll(h_im, d, 0), 0.0)
        s_re = sre_ref[k]
        s_im = sim_ref[k]
        h_re, h_im = h_re + (s_re * p_re - s_im * p_im), h_im + (s_re * p_im + s_im * p_re)
    hre_ref[0] = h_re[n - 1:n]
    him_ref[0] = h_im[n - 1:n]
    keep = pos >= 1
    g_re = jnp.where(keep, pltpu.roll(h_re, 1, 0), 0.0)
    g_im = jnp.where(keep, pltpu.roll(h_im, 1, 0), 0.0)
    for t in range(S5_CHUNK):
        g_re, g_im = a_re * g_re - a_im * g_im, a_re * g_im + a_im * g_re
        g = jnp.concatenate([g_re, g_im], axis=1).astype(BF16)
        y = _dot(u2_sc[:, 0:(t + 1) * w], kst_ref[(S5_CHUNK - 1 - t) * w:, :]) + _dot(g, c_ref[...])
        ylo_ref[pl.ds(t, n, stride=S5_CHUNK), :] = y[:, :LANE] + d_ref[:, :LANE] * token(ulo_ref, t)
        yhi_ref[pl.ds(t, n, stride=S5_CHUNK), :] = y[:, LANE:] + d_ref[:, LANE:] * token(uhi_ref, t)


def _s5_seq_call(u_lo, u_hi, dskip, wts, *, bsz, seq):
    cps = seq // S5_CHUNK
    names = ("b", "c", "kst", "a_re", "a_im", "s_re", "s_im")
    half = pl.BlockSpec((seq, LANE), lambda i: (i, 0))
    state_spec = pl.BlockSpec((1, 1, N_STATE), lambda i: (i, 0, 0))
    y_lo, y_hi, h_re, h_im = pl.pallas_call(
        _s5_seq_kernel,
        grid=(bsz,),
        in_specs=[half, half, _full(dskip)] + [_full(wts[k]) for k in names],
        out_specs=[half, half, state_spec, state_spec],
        out_shape=[jax.ShapeDtypeStruct(u_lo.shape, F32), jax.ShapeDtypeStruct(u_lo.shape, F32),
                   jax.ShapeDtypeStruct((bsz, 1, N_STATE), F32), jax.ShapeDtypeStruct((bsz, 1, N_STATE), F32)],
        scratch_shapes=[pltpu.VMEM((cps, S5_CHUNK * SSM_WIDTH), BF16)],
        compiler_params=_cparams(("arbitrary",)),
        name="s5_seq",
    )(*_raw([u_lo, u_hi, dskip] + [wts[k] for k in names]))
    state = lambda h: h.reshape(bsz, SSM_GROUPS, SSM_STATE)
    return y_lo, y_hi, state(h_re), state(h_im)


def _s5_step_kernel(u_ref, d_ref, h0re_ref, h0im_ref, are_ref, aim_ref, b_ref, c_ref, y_ref, hre_ref, him_ref, *,
                    n_new):
    h_re = h0re_ref[...]
    h_im = h0im_ref[...]
    a_re = are_ref[...]
    a_im = aim_ref[...]
    for t in range(n_new):
        ut = u_ref[t]
        x = _dot(ut.astype(BF16), b_ref[...])
        h_re, h_im = (a_re * h_re - a_im * h_im + x[:, :N_STATE], a_re * h_im + a_im * h_re + x[:, N_STATE:])
        y_ref[t] = _dot(jnp.concatenate([h_re, h_im], axis=1).astype(BF16), c_ref[...]) + d_ref[...] * ut
    hre_ref[...] = h_re
    him_ref[...] = h_im


def _s5_step_call(u, dskip, h0_re, h0_im, wts, *, bsz, n_new):
    u_t = jnp.swapaxes(u.reshape(bsz, n_new, SSM_WIDTH), 0, 1)
    ins = [u_t, dskip, h0_re, h0_im, wts["a_re"], wts["a_im"], wts["b"], wts["c"]]
    outs = [jax.ShapeDtypeStruct(u_t.shape, F32), jax.ShapeDtypeStruct((bsz, N_STATE), F32),
            jax.ShapeDtypeStruct((bsz, N_STATE), F32)]
    y, h_re, h_im = pl.pallas_call(
        functools.partial(_s5_step_kernel, n_new=n_new),
        grid=(1,),
        in_specs=[_full(a) for a in ins],
        out_specs=[_full(o) for o in outs],
        out_shape=outs,
        compiler_params=_cparams(("arbitrary",)),
        name="s5_step",
    )(*_raw(ins))
    y = jnp.swapaxes(y, 0, 1).reshape(bsz * n_new, SSM_WIDTH)
    return (y[:, :LANE], y[:, LANE:], h_re.reshape(bsz, SSM_GROUPS, SSM_STATE),
            h_im.reshape(bsz, SSM_GROUPS, SSM_STATE))


def _merge_kernel(x_ref, sh_ref, sc_ref, gt_ref, g1_ref, wg_ref, yslo_ref, yshi_ref, wglu_ref,
                  gated_ref, wgm_ref, att_ref, wat_ref, wo_ref, o_ref, *, att_feature_major):
    x = x_ref[...]
    h = (_rms(x) * g1_ref[...] * (1.0 + sc_ref[0]) + sh_ref[0]).astype(BF16)
    y_s = jnp.concatenate([yslo_ref[...], yshi_ref[...]], axis=1)
    glu = _dot(_gelu(y_s).astype(BF16), wglu_ref[...])
    d = D_MODEL
    y_att = _dot_tn(att_ref[0], wat_ref[...]) if att_feature_major else _dot(att_ref[...], wat_ref[...])
    merged = jax.nn.sigmoid(_dot(h, wg_ref[:, 0:d])) * (glu[:, :d] * jax.nn.sigmoid(glu[:, d:]))
    merged = merged + jax.nn.sigmoid(_dot(h, wg_ref[:, d:2 * d])) * _dot(gated_ref[...], wgm_ref[...])
    merged = merged + jax.nn.sigmoid(_dot(h, wg_ref[:, 2 * d:3 * d])) * y_att
    o_ref[...] = x + gt_ref[0] * _dot(merged.astype(BF16), wo_ref[...])


def _merge_call(x, ada, g1, wg, ys_lo, ys_hi, wglu, gated, wgm, att, wat, wo, *, tm, tiles_per_batch):
    t = x.shape[0]
    row = lambda n: pl.BlockSpec((tm, n), lambda i: (i, 0))
    fm = att.ndim == 3
    att_spec = (pl.BlockSpec((1, ATT_WIDTH, tm), lambda i: (i // tiles_per_batch, 0, i % tiles_per_batch))
                if fm else row(ATT_WIDTH))
    return pl.pallas_call(
        functools.partial(_merge_kernel, att_feature_major=fm),
        grid=(t // tm,),
        in_specs=[row(D_MODEL), _mod_spec(ada, tiles_per_batch, 0), _mod_spec(ada, tiles_per_batch, 1),
                  _mod_spec(ada, tiles_per_batch, 2), _full(g1), _full(wg), row(LANE), row(LANE),
                  _full(wglu), row(GMLP_WIDTH), _full(wgm), att_spec, _full(wat), _full(wo)],
        out_specs=row(D_MODEL),
        out_shape=jax.ShapeDtypeStruct(x.shape, F32),
        compiler_params=_cparams(("arbitrary",)),
        name="merge_out",
    )(*_raw([x, ada, ada, ada, g1, wg, ys_lo, ys_hi, wglu, gated, wgm, att, wat, wo]))


def _mlp_kernel(x_ref, sh_ref, sc_ref, gt_ref, g2_ref, wup_ref, wdn_ref, gfin_ref, o_ref, y_ref):
    x = x_ref[...]
    h = (_rms(x) * g2_ref[...] * (1.0 + sc_ref[0]) + sh_ref[0]).astype(BF16)
    blk = D_FF // FF_SPLIT
    acc = None
    for c in range(FF_SPLIT):
        a = jnp.maximum(_dot(h, wup_ref[:, c * blk:(c + 1) * blk]), 0.0)
        part = _dot((a * a).astype(BF16), wdn_ref[c * blk:(c + 1) * blk, :])
        acc = part if acc is None else acc + part
    out = x + gt_ref[0] * acc
    o_ref[...] = out
    y_ref[...] = _rms(out) * gfin_ref[...]


def _mlp_call(x, ada, g2, wup, wdn, gfin, *, tm, tiles_per_batch):
    t = x.shape[0]
    row = lambda n: pl.BlockSpec((tm, n), lambda i: (i, 0))
    return pl.pallas_call(
        _mlp_kernel,
        grid=(t // tm,),
        in_specs=[row(D_MODEL), _mod_spec(ada, tiles_per_batch, 3), _mod_spec(ada, tiles_per_batch, 4),
                  _mod_spec(ada, tiles_per_batch, 5), _full(g2), _full(wup), _full(wdn), _full(gfin)],
        out_specs=[row(D_MODEL), row(D_MODEL)],
        out_shape=[jax.ShapeDtypeStruct(x.shape, F32), jax.ShapeDtypeStruct(x.shape, F32)],
        compiler_params=_cparams(("arbitrary",)),
        name="mlp",
    )(*_raw([x, ada, ada, ada, g2, wup, wdn, gfin]))


def _mix_weights(w_s, b_s, seq, rows):
    cl = min(seq, CHUNK)
    depth = w_s.shape[0]
    w = jnp.tril(w_s[:, :, :cl, :cl])
    reps = rows // cl
    eye = jnp.eye(reps, dtype=w.dtype)
    wmix = jnp.einsum("lgts,ab->lgatbs", w, eye).reshape(depth, GMLP_GROUPS, rows, rows).astype(BF16)
    bias = jnp.tile(jnp.swapaxes(b_s[:, :, :cl], 1, 2), (1, reps, 1))
    bmix = jnp.repeat(bias, GMLP_GROUP_CH, axis=2)
    return wmix, bmix


def _cumsum_tri(rows, seq):
    r = jnp.arange(rows)
    return ((r[:, None] >= r[None, :]) & (r[:, None] // seq == r[None, :] // seq)).astype(BF16)


def kernel(x_prompt, x_sample, c_prompt, c_sample, cache_k, cache_v, cache_logf, state_ssm_re, state_ssm_im, page_table, w_ada, b_ada, g_norm1, w_in, b_f, ssm_a_re, ssm_a_im, ssm_log_dt, ssm_b_re, ssm_b_im, ssm_c_re, ssm_c_im, ssm_d, w_glu, g_gv, w_s, b_s, w_gmlp_out, w_att_out, w_o, g_norm2, w_up, w_down, g_final):
    depth = w_in.shape[0]
    bp, lp, d = x_prompt.shape
    bs, ls, _ = x_sample.shape
    tp, ts = bp * lp, bs * ls
    tm_p = TM_PROMPT
    tiles_pb = lp // tm_p

    ada = _ada_call(jnp.concatenate([c_prompt, c_sample], axis=0), w_ada, b_ada)
    ada_p = ada[:, :bp].reshape(depth, bp, 1, 6 * d)
    ada_s = jnp.repeat(ada[:, bp:], ls, axis=1).reshape(depth, 1, ts, 6 * d)
    n_pool, page = cache_k.shape[1], cache_k.shape[2]
    feature_major = lambda c: jnp.transpose(c, (0, 1, 3, 4, 2)).reshape(depth, n_pool, ATT_WIDTH, page)
    cache_kt = feature_major(cache_k)
    cache_vt = feature_major(cache_v)
    cache_lf_t = jnp.swapaxes(cache_logf, 2, 3)

    pad_f = lambda w: jnp.pad(w, ((0, 0), (0, 0), (0, LANE - ATT_HEADS)))
    w_qkv, w_f, w_sgg = w_in[:, :, Q_OFF:F_OFF], w_in[:, :, F_OFF:S_OFF], w_in[:, :, S_OFF:G_OFF]
    stacked = dict(
        w_row_p=jnp.concatenate([w_in[:, :, K_OFF:V_OFF], w_sgg, pad_f(w_f)], axis=2).astype(BF16),
        w_t_p=jnp.pad(jnp.swapaxes(jnp.concatenate([w_qkv, w_f], axis=2), 1, 2),
                      ((0, 0), (0, BF16_ROWS - ATT_HEADS), (0, 0))).astype(BF16),
        w_row_s=jnp.concatenate([w_qkv, w_sgg, pad_f(w_f)], axis=2).astype(BF16),
        w_gate=w_in[:, :, G_OFF:].astype(BF16),
        bf_row=jnp.pad(b_f, ((0, 0), (0, LANE - ATT_HEADS))).reshape(depth, 1, LANE),
        bf_col=b_f.reshape(depth, ATT_HEADS, 1),
        g1=g_norm1.reshape(depth, 1, d), g2=g_norm2.reshape(depth, 1, d),
        ggv=g_gv.reshape(depth, 1, GMLP_WIDTH), dskip=ssm_d.reshape(depth, 1, SSM_WIDTH),
        wglu=w_glu.astype(BF16), wgm=w_gmlp_out.astype(BF16), wat=w_att_out.astype(BF16), wo=w_o.astype(BF16),
        wup=w_up.astype(BF16), wdn=w_down.astype(BF16),
        h0_re=state_ssm_re.reshape(depth, bs, N_STATE), h0_im=state_ssm_im.reshape(depth, bs, N_STATE),
        ada_p=ada_p, ada_s=ada_s)
    stacked["wmix_p"], stacked["bmix_p"] = _mix_weights(w_s, b_s, lp, CHUNK)
    stacked["wmix_s"], stacked["bmix_s"] = _mix_weights(w_s, b_s, ls, ts)
    s5_stacked = jax.vmap(functools.partial(_s5_weights, chunks_per_seq=lp // S5_CHUNK))(
        ssm_a_re, ssm_a_im, ssm_log_dt, ssm_b_re, ssm_b_im, ssm_c_re, ssm_c_im)
    tri_p = _cumsum_tri(tm_p, tm_p)
    tri_s = _cumsum_tri(ts, ls)
    gfin = g_final.reshape(1, d)

    xp = x_prompt.reshape(tp, d)
    xs = x_sample.reshape(ts, d)
    outs = {k: [] for k in ("kp", "vp", "lfp", "hrp", "hip", "gvp", "ks", "vs", "lfs", "hrs", "his", "gvs")}
    yp = ys = None
    chunk_start = ((lp - 1) // CHUNK) * CHUNK
    token_major = lambda a: jnp.transpose(a.reshape(bp, ATT_HEADS, HEAD_DIM, lp), (0, 3, 1, 2))
    for l in range(depth):
        w = {k: _Layer(v, l) for k, v in stacked.items()}
        s5w = {k: _Layer(v, l) for k, v in s5_stacked.items()}

        qt, ktf, vtf, vtb, kb, lft, cumt, fkb, u_lo, u_hi, gated, gv = _in_proj_p_call(
            xp, w["ada_p"], w["g1"], w["w_row_p"], w["w_t_p"], w["bf_row"], w["bf_col"], w["ggv"], w["wmix_p"],
            w["bmix_p"], tri_p, tri_p.T, bsz=bp, seq=lp, tm=tm_p)
        att_t = _flash_call(kb, qt, vtb, fkb, cumt, bsz=bp, seq=lp, tq=tm_p)
        ys_lo, ys_hi, hr, hi = _s5_seq_call(u_lo, u_hi, w["dskip"], s5w, bsz=bp, seq=lp)
        xp = _merge_call(xp, w["ada_p"], w["g1"], w["w_gate"], ys_lo, ys_hi, w["wglu"], gated, w["wgm"], att_t,
                         w["wat"], w["wo"], tm=tm_p, tiles_per_batch=tiles_pb)
        xp, yp = _mlp_call(xp, w["ada_p"], w["g2"], w["wup"], w["wdn"], gfin, tm=tm_p, tiles_per_batch=tiles_pb)
        outs["kp"].append(token_major(ktf))
        outs["vp"].append(token_major(vtf))
        outs["lfp"].append(jnp.swapaxes(lft, 1, 2))
        outs["hrp"].append(hr)
        outs["hip"].append(hi)
        outs["gvp"].append(gv.reshape(bp, lp, GMLP_WIDTH)[:, chunk_start:])

        q, kf, kb, vf, vb, lf, cum, u, gated, gv = _in_proj_s_call(
            xs, w["ada_s"], w["g1"], w["w_row_s"], w["bf_row"], w["ggv"], w["wmix_s"], w["bmix_s"], tri_s)
        cum_t = jnp.swapaxes(cum.reshape(bs, ls, ATT_HEADS), 1, 2)
        att = _decode_call(page_table, q, kb, vb, cum, cum_t, cache_kt, cache_vt, cache_lf_t,
                           layer=l, bsz=bs, n_new=ls)
        ys_lo, ys_hi, hr, hi = _s5_step_call(u, w["dskip"], w["h0_re"], w["h0_im"], s5w, bsz=bs, n_new=ls)
        xs = _merge_call(xs, w["ada_s"], w["g1"], w["w_gate"], ys_lo, ys_hi, w["wglu"], gated, w["wgm"], att,
                         w["wat"], w["wo"], tm=ts, tiles_per_batch=1)
        xs, ys = _mlp_call(xs, w["ada_s"], w["g2"], w["wup"], w["wdn"], gfin, tm=ts, tiles_per_batch=1)
        outs["ks"].append(kf.reshape(bs, ls, ATT_HEADS, HEAD_DIM))
        outs["vs"].append(vf.reshape(bs, ls, ATT_HEADS, HEAD_DIM))
        outs["lfs"].append(lf.reshape(bs, ls, ATT_HEADS))
        outs["hrs"].append(hr)
        outs["his"].append(hi)
        outs["gvs"].append(gv.reshape(bs, ls, GMLP_WIDTH))

    st = lambda k: jnp.stack(outs[k])
    return (yp.reshape(bp, lp, d), ys.reshape(bs, ls, d), st("kp"), st("vp"), st("lfp"), st("hrp"), st("hip"), st("gvp"),
            st("ks"), st("vs"), st("lfs"), st("hrs"), st("his"), st("gvs"))
```

```python
import functools
import math

import jax
import jax.numpy as jnp
from jax import lax
from jax.experimental import pallas as pl
from jax.experimental.pallas import tpu as pltpu

F32 = jnp.float32
BF16 = jnp.bfloat16

D_MODEL = 1024
ATT_HEADS = 8
HEAD_DIM = 64
ATT_WIDTH = ATT_HEADS * HEAD_DIM
SSM_GROUPS = 16
SSM_GROUP_CH = 16
SSM_WIDTH = SSM_GROUPS * SSM_GROUP_CH
SSM_STATE = 64
N_STATE = SSM_GROUPS * SSM_STATE
GMLP_GROUPS = 4
GMLP_GROUP_CH = 64
GMLP_WIDTH = GMLP_GROUPS * GMLP_GROUP_CH
CHUNK = 128
N_BRANCH = 3
D_FF = 4 * D_MODEL
EPS = 1e-6

Q_OFF = 0
K_OFF = Q_OFF + ATT_WIDTH
V_OFF = K_OFF + ATT_WIDTH
F_OFF = V_OFF + ATT_WIDTH
S_OFF = F_OFF + ATT_HEADS
GU_OFF = S_OFF + SSM_WIDTH
GV_OFF = GU_OFF + GMLP_WIDTH
G_OFF = GV_OFF + GMLP_WIDTH

LOG2E = math.log2(math.e)
LANE = 128
BF16_ROWS = 16
S5_CHUNK = 16
TM_PROMPT = 512
FLASH_HEADS = 8
DEC_PAGES = 8
DEC_SLOTS = 4
FF_SPLIT = 4
VMEM_LIMIT = 56 * 1024 * 1024


def _cparams(sem):
    return pltpu.CompilerParams(dimension_semantics=sem, vmem_limit_bytes=VMEM_LIMIT)


def _dot(a, b):
    return jnp.dot(a, b, preferred_element_type=F32)


def _dot_nt(a, b):
    return lax.dot_general(a, b, (((1,), (1,)), ((), ())), preferred_element_type=F32)


def _dot_tn(a, b):
    return lax.dot_general(a, b, (((0,), (0,)), ((), ())), preferred_element_type=F32)


def _split3(x):
    hi = x.astype(BF16)
    r1 = x - hi.astype(F32)
    mid = r1.astype(BF16)
    lo = (r1 - mid.astype(F32)).astype(BF16)
    return hi, mid, lo


def _gelu(x):
    return 0.5 * x * (1.0 + jnp.tanh(math.sqrt(2.0 / math.pi) * (x + 0.044715 * (x * x * x))))


def _log_sigmoid(x):
    return -(jnp.maximum(-x, 0.0) + jnp.log1p(jnp.exp(-jnp.abs(x))))


def _rms(x):
    return x * lax.rsqrt(jnp.mean(x * x, axis=-1, keepdims=True) + EPS)


class _Layer:
    def __init__(self, stacked, index):
        self.stacked, self.index = stacked, index

    @property
    def shape(self):
        return self.stacked.shape[1:]

    @property
    def ndim(self):
        return self.stacked.ndim - 1


def _raw(args):
    return [a.stacked if isinstance(a, _Layer) else a for a in args]


def _full(arr):
    if isinstance(arr, _Layer):
        l, shape, zeros = arr.index, arr.shape, (0,) * arr.ndim
        return pl.BlockSpec((None,) + shape, lambda *_: (l,) + zeros)
    return pl.BlockSpec(arr.shape, lambda *_: (0,) * arr.ndim)


def _ada_kernel(c_ref, w_ref, b_ref, o_ref):
    c = c_ref[...]
    a = (c * jax.nn.sigmoid(c)).astype(BF16)
    o_ref[0] = _dot(a, w_ref[0].astype(BF16)) + b_ref[0]


def _ada_call(c_all, w_ada, b_ada):
    depth, d, n = w_ada.shape
    r = c_all.shape[0]
    tn = 1536
    return pl.pallas_call(
        _ada_kernel,
        grid=(depth, n // tn),
        in_specs=[pl.BlockSpec((r, d), lambda l, j: (0, 0)),
                  pl.BlockSpec((1, d, tn), lambda l, j: (l, 0, j)),
                  pl.BlockSpec((1, 1, tn), lambda l, j: (l, 0, j))],
        out_specs=pl.BlockSpec((1, r, tn), lambda l, j: (l, 0, j)),
        out_shape=jax.ShapeDtypeStruct((depth, r, n), F32),
        compiler_params=_cparams(("arbitrary", "arbitrary")),
        name="ada",
    )(c_all, w_ada, b_ada.reshape(depth, 1, n))


def _mod_spec(ada, tiles_per_batch, j):
    l, r = ada.index, ada.shape[1]
    return pl.BlockSpec((None, 1, r, D_MODEL), lambda i: (l, i // tiles_per_batch, 0, j))


def _gmlp_gate(gu, gv, wmix_ref, bmix_ref, gated_ref, mix_rows):
    lane = lax.broadcasted_iota(jnp.int32, (1, GMLP_WIDTH), 1)
    for c in range(gu.shape[0] // mix_rows):
        rows = slice(c * mix_rows, (c + 1) * mix_rows)
        gvc = gv[rows]
        s = bmix_ref[...]
        for g in range(GMLP_GROUPS):
            in_g = (lane >= g * GMLP_GROUP_CH) & (lane < (g + 1) * GMLP_GROUP_CH)
            s = s + _dot(wmix_ref[g], jnp.where(in_g, gvc, 0.0).astype(BF16))
        gated_ref[rows, :] = (gu[rows] * s).astype(BF16)


W_ROW_COLS = ATT_WIDTH + SSM_WIDTH + 2 * GMLP_WIDTH + LANE
W_T_ROWS = 3 * ATT_WIDTH + BF16_ROWS


def _in_proj_p_kernel(x_ref, sh_ref, sc_ref, g1_ref, wr_ref, wt_ref, bfr_ref, bfc_ref, ggv_ref, wmix_ref, bmix_ref,
                      tri_ref, triu_ref, pk_ref, onesk_ref, pq_ref, onesq_ref,
                      qt_ref, ktf_ref, vtf_ref, vtb_ref, kb_ref, lft_ref, kaug_ref, qaug_ref, ulo_ref, uhi_ref,
                      gated_ref, gv_ref, carry_r, carry_c, *, tiles_per_batch, mix_rows):
    i = pl.program_id(0)
    tm = x_ref.shape[0]
    a = ATT_WIDTH
    h = (_rms(x_ref[...]) * g1_ref[...] * (1.0 + sc_ref[0]) + sh_ref[0]).astype(BF16)

    @pl.when(i % tiles_per_batch == 0)
    def _():
        carry_r[...] = jnp.zeros_like(carry_r)
        carry_c[...] = jnp.zeros_like(carry_c)

    zt = _dot_nt(wt_ref[...], h)
    qt_ref[0] = (zt[0:a] * (LOG2E * HEAD_DIM ** -0.5)).astype(BF16)
    ktf_ref[0] = zt[a:2 * a]
    vt = zt[2 * a:3 * a]
    vtf_ref[0] = vt
    vtb_ref[0] = vt.astype(BF16)
    lft = _log_sigmoid(zt[3 * a:3 * a + ATT_HEADS] + bfc_ref[...])
    lft_ref[0] = lft
    hi, mid, lo = _split3(lft)
    triu = triu_ref[...]
    cumt = (_dot(hi, triu) + _dot(mid, triu)) + _dot(lo, triu) + carry_c[...]
    carry_c[...] = cumt[:, tm - 1:tm]
    parts = [t.astype(F32) for t in _split3(cumt * LOG2E)] + [jnp.zeros_like(cumt)]
    qaug_ref[0] = (_dot(pq_ref[...], jnp.concatenate(parts, axis=0).astype(BF16)) + onesq_ref[...]).astype(BF16)

    kb_ref[...] = _dot(h, wr_ref[:, 0:a]).astype(BF16)
    o = a
    u = _dot(h, wr_ref[:, o:o + SSM_WIDTH])
    ulo_ref[...] = u[:, :LANE]
    uhi_ref[...] = u[:, LANE:]
    o += SSM_WIDTH
    gu = _gelu(_dot(h, wr_ref[:, o:o + GMLP_WIDTH]))
    o += GMLP_WIDTH
    gv = _rms(_gelu(_dot(h, wr_ref[:, o:o + GMLP_WIDTH]))) * ggv_ref[...]
    gv_ref[0] = gv[tm - mix_rows:tm]
    o += GMLP_WIDTH

    lf = _log_sigmoid(_dot(h, wr_ref[:, o:o + LANE]) + bfr_ref[...])
    hi, mid, lo = _split3(lf)
    tri = tri_ref[...]
    cum = (_dot(tri, hi) + _dot(tri, mid)) + _dot(tri, lo) + carry_r[...]
    carry_r[...] = cum[tm - 1:tm, :]
    hi, mid, lo = _split3(cum * LOG2E)
    kaug_ref[...] = ((_dot(hi, pk_ref[0]) + _dot(mid, pk_ref[1])) + _dot(lo, pk_ref[2]) + onesk_ref[...]).astype(BF16)

    _gmlp_gate(gu, gv, wmix_ref, bmix_ref, gated_ref, mix_rows)


def _in_proj_p_call(x, ada, g1, wr, wt, bfr, bfc, ggv, wmix, bmix, tri, triu, aug, *, bsz, seq, tm):
    t = x.shape[0]
    tpb = seq // tm
    row = lambda n: pl.BlockSpec((tm, n), lambda i: (i, 0))
    fm = lambda n: pl.BlockSpec((1, n, tm), lambda i: (i // tpb, 0, i % tpb))
    outs = [(fm(ATT_WIDTH), (bsz, ATT_WIDTH, seq), BF16),
            (fm(ATT_WIDTH), (bsz, ATT_WIDTH, seq), F32),
            (fm(ATT_WIDTH), (bsz, ATT_WIDTH, seq), F32),
            (fm(ATT_WIDTH), (bsz, ATT_WIDTH, seq), BF16),
            (row(ATT_WIDTH), (t, ATT_WIDTH), BF16),
            (fm(ATT_HEADS), (bsz, ATT_HEADS, seq), F32),
            (row(LANE), (t, LANE), BF16),
            (fm(LANE), (bsz, LANE, seq), BF16),
            (row(LANE), (t, LANE), F32),
            (row(LANE), (t, LANE), F32),
            (row(GMLP_WIDTH), (t, GMLP_WIDTH), BF16),
            (pl.BlockSpec((1, CHUNK, GMLP_WIDTH), lambda i: (i // tpb, 0, 0)),
             (bsz, CHUNK, GMLP_WIDTH), F32)]
    assert seq % CHUNK == 0 and wmix.shape[1] == CHUNK
    return pl.pallas_call(
        functools.partial(_in_proj_p_kernel, tiles_per_batch=tpb, mix_rows=wmix.shape[1]),
        grid=(t // tm,),
        in_specs=[row(D_MODEL), _mod_spec(ada, tpb, 0), _mod_spec(ada, tpb, 1), _full(g1), _full(wr), _full(wt),
                  _full(bfr), _full(bfc), _full(ggv), _full(wmix), _full(bmix), _full(tri), _full(triu)]
                 + [_full(a) for a in aug],
        out_specs=[o[0] for o in outs],
        out_shape=[jax.ShapeDtypeStruct(o[1], o[2]) for o in outs],
        scratch_shapes=[pltpu.VMEM((1, LANE), F32), pltpu.VMEM((ATT_HEADS, 1), F32)],
        compiler_params=_cparams(("arbitrary",)),
        name="in_proj_p",
    )(*_raw([x, ada, ada, g1, wr, wt, bfr, bfc, ggv, wmix, bmix, tri, triu, *aug]))


W_S_COLS = 3 * ATT_WIDTH + SSM_WIDTH + 2 * GMLP_WIDTH + LANE


def _in_proj_s_kernel(x_ref, sh_ref, sc_ref, g1_ref, w_ref, bf_ref, ggv_ref, wmix_ref, bmix_ref, tri_ref,
                      q_ref, kf_ref, kb_ref, vf_ref, vb_ref, lf_ref, cum_ref, u_ref, gated_ref, gv_ref, *, mix_rows):
    h = (_rms(x_ref[...]) * g1_ref[...] * (1.0 + sc_ref[0]) + sh_ref[0]).astype(BF16)
    a = ATT_WIDTH
    q_ref[...] = (_dot(h, w_ref[:, 0:a]) * (HEAD_DIM ** -0.5)).astype(BF16)
    k = _dot(h, w_ref[:, a:2 * a])
    kf_ref[...] = k
    kb_ref[...] = k.astype(BF16)
    v = _dot(h, w_ref[:, 2 * a:3 * a])
    vf_ref[...] = v
    vb_ref[...] = v.astype(BF16)
    o = 3 * a
    u_ref[...] = _dot(h, w_ref[:, o:o + SSM_WIDTH])
    o += SSM_WIDTH
    gu = _gelu(_dot(h, w_ref[:, o:o + GMLP_WIDTH]))
    o += GMLP_WIDTH
    gv = _rms(_gelu(_dot(h, w_ref[:, o:o + GMLP_WIDTH]))) * ggv_ref[...]
    gv_ref[...] = gv
    o += GMLP_WIDTH
    lf = _log_sigmoid(_dot(h, w_ref[:, o:o + LANE]) + bf_ref[...])
    lf_ref[...] = lf[:, :ATT_HEADS]
    hi, mid, lo = _split3(lf)
    tri = tri_ref[...]
    cum = (_dot(tri, hi) + _dot(tri, mid)) + _dot(tri, lo)
    cum_ref[...] = cum[:, :ATT_HEADS]
    _gmlp_gate(gu, gv, wmix_ref, bmix_ref, gated_ref, mix_rows)


def _in_proj_s_call(x, ada, g1, w, bf, ggv, wmix, bmix, tri):
    t = x.shape[0]
    row = lambda n: pl.BlockSpec((t, n), lambda i: (0, 0))
    outs = [(ATT_WIDTH, BF16), (ATT_WIDTH, F32), (ATT_WIDTH, BF16), (ATT_WIDTH, F32), (ATT_WIDTH, BF16),
            (ATT_HEADS, F32), (ATT_HEADS, F32), (SSM_WIDTH, F32), (GMLP_WIDTH, BF16), (GMLP_WIDTH, F32)]
    return pl.pallas_call(
        functools.partial(_in_proj_s_kernel, mix_rows=wmix.shape[1]),
        grid=(1,),
        in_specs=[row(D_MODEL), _mod_spec(ada, 1, 0), _mod_spec(ada, 1, 1),
                  _full(g1), _full(w), _full(bf), _full(ggv), _full(wmix), _full(bmix), _full(tri)],
        out_specs=[row(n) for n, _ in outs],
        out_shape=[jax.ShapeDtypeStruct((t, n), dt) for n, dt in outs],
        compiler_params=_cparams(("arbitrary",)),
        name="in_proj_s",
    )(*_raw([x, ada, ada, g1, w, bf, ggv, wmix, bmix, tri]))


AUG_ROWS = 16
VT_ROWS = HEAD_DIM + BF16_ROWS


def _flash_kernel(qi_tab, ki_tab, k_ref, qt_ref, vt_ref, kaug_ref, qaug_ref, o_ref, m_sc, acc_sc):
    hg = pl.program_id(1)
    s_idx = pl.program_id(2)
    qi = qi_tab[s_idx]
    ki = ki_tab[s_idx]
    tk = k_ref.shape[0]
    tq = qt_ref.shape[2]
    pair = 2 * HEAD_DIM

    @pl.when(ki == 0)
    def _():
        m_sc[...] = jnp.full_like(m_sc, -jnp.inf)
        acc_sc[...] = jnp.zeros_like(acc_sc)

    def step(masked):
        feat = lax.broadcasted_iota(jnp.int32, (pair, 1), 0)
        kaug = kaug_ref[...]
        qaug = qaug_ref[0]
        ones_tile = (lax.broadcasted_iota(jnp.int32, (BF16_ROWS, tk), 0) == 0).astype(BF16)
        sts = []
        for h in range(FLASH_HEADS):
            hp, hh = divmod(h, 2)
            head = hg * FLASH_HEADS + h
            k = jnp.concatenate([k_ref[:, hp * pair:(hp + 1) * pair], kaug], axis=1)
            qt = qt_ref[0, hp * pair:(hp + 1) * pair, :]
            own = (feat >= hh * HEAD_DIM) & (feat < (hh + 1) * HEAD_DIM)
            own_aug = (feat >= head * AUG_ROWS) & (feat < (head + 1) * AUG_ROWS)
            qfull = jnp.concatenate([jnp.where(own, qt, jnp.zeros_like(qt)),
                                     jnp.where(own_aug, qaug, jnp.zeros_like(qaug))], axis=0)
            st = _dot(k, qfull)
            if masked:
                kj = lax.broadcasted_iota(jnp.int32, (tk, tq), 0)
                qc = lax.broadcasted_iota(jnp.int32, (tk, tq), 1)
                st = jnp.where(kj <= qc, st, -jnp.inf)
            sts.append(st)
        for h in range(FLASH_HEADS):
            st = sts[h]
            m_prev = m_sc[h]
            m_new = jnp.maximum(m_prev, jnp.max(st, axis=0, keepdims=True))
            alpha = jnp.exp2(m_prev - m_new)
            p = jnp.exp2(st - m_new).astype(BF16)
            vt = jnp.concatenate([vt_ref[0, h * HEAD_DIM:(h + 1) * HEAD_DIM, :], ones_tile], axis=0)
            acc_sc[h] = alpha * acc_sc[h] + _dot(vt, p)
            m_sc[h] = m_new

    @pl.when(ki < qi)
    def _():
        step(False)

    @pl.when(ki == qi)
    def _():
        step(True)
        for h in range(FLASH_HEADS):
            acc = acc_sc[h]
            o_ref[0, h * HEAD_DIM:(h + 1) * HEAD_DIM, :] = (acc[:HEAD_DIM] / acc[HEAD_DIM:HEAD_DIM + 1]).astype(o_ref.dtype)


def _flash_call(k, qt, vt, kaug, qaug, *, bsz, seq, tq):
    nq = seq // tq
    tri = [(a, b) for a in range(nq) for b in range(a + 1)]
    qi_tab = jnp.array([a for a, _ in tri], jnp.int32)
    ki_tab = jnp.array([b for _, b in tri], jnp.int32)
    width = FLASH_HEADS * HEAD_DIM
    q_spec = pl.BlockSpec((1, width, tq), lambda b, hg, s, qt_, kt_: (b, hg, qt_[s]))
    return pl.pallas_call(
        _flash_kernel,
        grid_spec=pltpu.PrefetchScalarGridSpec(
            num_scalar_prefetch=2,
            grid=(bsz, ATT_HEADS // FLASH_HEADS, len(tri)),
            in_specs=[pl.BlockSpec((tq, width), lambda b, hg, s, qt_, kt_: (b * nq + kt_[s], hg)),
                      q_spec,
                      pl.BlockSpec((1, width, tq), lambda b, hg, s, qt_, kt_: (b, hg, kt_[s])),
                      pl.BlockSpec((tq, LANE), lambda b, hg, s, qt_, kt_: (b * nq + kt_[s], 0)),
                      pl.BlockSpec((1, LANE, tq), lambda b, hg, s, qt_, kt_: (b, 0, qt_[s]))],
            out_specs=q_spec,
            scratch_shapes=[pltpu.VMEM((FLASH_HEADS, 1, tq), F32), pltpu.VMEM((FLASH_HEADS, VT_ROWS, tq), F32)]),
        out_shape=jax.ShapeDtypeStruct(qt.shape, BF16),
        compiler_params=_cparams(("arbitrary", "arbitrary", "arbitrary")),
        name="flash",
    )(qi_tab, ki_tab, k, qt, vt, kaug, qaug)


def _decode_kernel(pt_ref, q_ref, kn_ref, vn_ref, cn_ref, cnt_ref, hmask_ref, qmask_ref, nmask_ref, tri_ref,
                   ck_hbm, cv_hbm, clf_hbm, o_ref, kbuf, vbuf, lbuf, sem, *, layer, n_pages):
    b = pl.program_id(0)
    nb = pl.num_programs(0)
    page = lbuf.shape[3]
    n_chunks = n_pages // DEC_PAGES
    n_q = q_ref.shape[0]
    rows = n_q * ATT_HEADS

    def copies(bb, chunk, slot):
        out = []
        for j in range(DEC_PAGES):
            pg = pt_ref[bb, chunk * DEC_PAGES + j]
            cols = pl.ds(j * page, page)
            out.append(pltpu.make_async_copy(ck_hbm.at[layer, pg], kbuf.at[slot, :, cols], sem.at[slot, 0]))
            out.append(pltpu.make_async_copy(cv_hbm.at[layer, pg], vbuf.at[slot, :, cols], sem.at[slot, 1]))
            out.append(pltpu.make_async_copy(clf_hbm.at[layer, pg], lbuf.at[slot, j], sem.at[slot, 2]))
        return out

    def chunk_of(step):
        return n_chunks - 1 - step

    ahead = DEC_SLOTS - 1

    @pl.when(b == 0)
    def _():
        for step in range(ahead):
            for cp in copies(b, chunk_of(step), step % DEC_SLOTS):
                cp.start()

    q = q_ref[...].astype(F32)
    qbd = jnp.broadcast_to(q[:, None, :], (n_q, ATT_HEADS, ATT_WIDTH)).reshape(rows, ATT_WIDTH)
    qbd = jnp.where(hmask_ref[...] > 0, qbd, 0.0).astype(BF16)
    cn = cn_ref[...]
    cn_rows = jnp.broadcast_to(cn[:, None, :], (n_q, ATT_HEADS, ATT_HEADS)).reshape(rows, ATT_HEADS)
    fq = jnp.sum(jnp.where(qmask_ref[...] > 0, cn_rows, 0.0), axis=-1, keepdims=True)

    s = _dot_nt(qbd, kn_ref[...])
    fk_new = jnp.concatenate([cnt_ref[0]] * n_q, axis=0)
    s = jnp.where(nmask_ref[...] > 0, s + fq - fk_new, -jnp.inf)
    m = jnp.max(s, axis=-1, keepdims=True)
    p = jnp.exp(s - m)
    l = jnp.sum(p, axis=-1, keepdims=True)
    acc = _dot(p.astype(BF16), vn_ref[...])
    run = jnp.zeros((ATT_HEADS, 1), F32)

    for step in range(n_chunks):
        slot = step % DEC_SLOTS
        for cp in copies(b, chunk_of(step), slot):
            cp.wait()
        nxt = step + ahead
        if nxt < n_chunks:
            for cp in copies(b, chunk_of(nxt), nxt % DEC_SLOTS):
                cp.start()
        else:
            @pl.when(b + 1 < nb)
            def _(nxt=nxt):
                for cp in copies(b + 1, chunk_of(nxt - n_chunks), nxt % DEC_SLOTS):
                    cp.start()

        kc = kbuf[slot].astype(BF16)
        vc = vbuf[slot].astype(BF16)
        lf = lbuf[slot].reshape(DEC_PAGES * ATT_HEADS, page)
        hi, mid, lo = _split3(lf)
        tri = tri_ref[...]
        suf = (_dot(hi, tri) + _dot(mid, tri)) + _dot(lo, tri)
        tot = jnp.sum(lf, axis=-1, keepdims=True)
        bias = []
        for j in reversed(range(DEC_PAGES)):
            pr = slice(j * ATT_HEADS, (j + 1) * ATT_HEADS)
            bias.append(suf[pr] + run)
            run = run + tot[pr]
        bias = jnp.concatenate(bias[::-1], axis=1)
        s = _dot(qbd, kc) + fq + jnp.concatenate([bias] * n_q, axis=0)
        m_new = jnp.maximum(m, jnp.max(s, axis=-1, keepdims=True))
        alpha = jnp.exp(m - m_new)
        p = jnp.exp(s - m_new)
        l = alpha * l + jnp.sum(p, axis=-1, keepdims=True)
        acc = alpha * acc + _dot_nt(p.astype(BF16), vc)
        m = m_new

    out = jnp.where(hmask_ref[...] > 0, acc / l, 0.0)
    o_ref[...] = jnp.sum(out.reshape(n_q, ATT_HEADS, ATT_WIDTH), axis=1).astype(o_ref.dtype)


def _decode_call(page_table, q, k_new, v_new, cn, cn_t, cache_kt, cache_vt, cache_lf_t, *, layer, bsz, n_new):
    n_pages = page_table.shape[1]
    assert n_pages % (DEC_PAGES * DEC_SLOTS) == 0
    page = cache_kt.shape[3]
    rows = n_new * ATT_HEADS
    r = jnp.arange(rows)
    hmask = (r[:, None] % ATT_HEADS == jnp.arange(ATT_WIDTH)[None, :] // HEAD_DIM).astype(F32)
    qmask = (r[:, None] % ATT_HEADS == jnp.arange(ATT_HEADS)[None, :]).astype(F32)
    nmask = (r[:, None] // ATT_HEADS >= jnp.arange(n_new)[None, :]).astype(F32)
    tri = (jnp.arange(page)[:, None] > jnp.arange(page)[None, :]).astype(BF16)
    tok = lambda n: pl.BlockSpec((n_new, n), lambda b, pt: (b, 0))
    hbm = pl.BlockSpec(memory_space=pl.ANY)
    return pl.pallas_call(
        functools.partial(_decode_kernel, layer=layer, n_pages=n_pages),
        grid_spec=pltpu.PrefetchScalarGridSpec(
            num_scalar_prefetch=1,
            grid=(bsz,),
            in_specs=[tok(ATT_WIDTH), tok(ATT_WIDTH), tok(ATT_WIDTH), tok(ATT_HEADS),
                      pl.BlockSpec((1, ATT_HEADS, n_new), lambda b, pt: (b, 0, 0)),
                      _full(hmask), _full(qmask), _full(nmask), _full(tri), hbm, hbm, hbm],
            out_specs=tok(ATT_WIDTH),
            scratch_shapes=[pltpu.VMEM((DEC_SLOTS, ATT_WIDTH, DEC_PAGES * page), F32),
                            pltpu.VMEM((DEC_SLOTS, ATT_WIDTH, DEC_PAGES * page), F32),
                            pltpu.VMEM((DEC_SLOTS, DEC_PAGES, ATT_HEADS, page), F32),
                            pltpu.SemaphoreType.DMA((DEC_SLOTS, 3))]),
        out_shape=jax.ShapeDtypeStruct(q.shape, BF16),
        compiler_params=_cparams(("arbitrary",)),
        name="decode",
    )(page_table, q, k_new, v_new, cn, cn_t, hmask, qmask, nmask, tri, cache_kt, cache_vt, cache_lf_t)


def _s5_discretise(a_re, a_im, log_dt, b_re, b_im):
    dt = jnp.exp(log_dt)[:, None]
    rate = a_re * dt
    ang = a_im * dt
    mag = jnp.exp(rate)
    ab_re = mag * jnp.cos(ang)
    ab_im = mag * jnp.sin(ang)
    den = a_re * a_re + a_im * a_im
    z_re = ((ab_re - 1.0) * a_re + ab_im * a_im) / den
    z_im = (ab_im * a_re - (ab_re - 1.0) * a_im) / den
    bb_re = z_re[..., None] * b_re - z_im[..., None] * b_im
    bb_im = z_re[..., None] * b_im + z_im[..., None] * b_re
    return rate, ang, ab_re, ab_im, bb_re, bb_im


def _abar_pow(rate, ang, j):
    mag = jnp.exp(rate * j)
    return mag * jnp.cos(ang * j), mag * jnp.sin(ang * j)


def _group_diag(m):
    g, r, c = m.shape
    return jnp.einsum("grc,gh->grhc", m, jnp.eye(g, dtype=m.dtype)).reshape(g * r, g * c)


def _s5_weights(a_re, a_im, log_dt, b_re, b_im, c_re, c_im, chunks_per_seq):
    rate, ang, ab_re, ab_im, bb_re, bb_im = _s5_discretise(a_re, a_im, log_dt, b_re, b_im)
    c = S5_CHUNK
    bfull = jnp.concatenate([_group_diag(jnp.swapaxes(bb_re, 1, 2)), _group_diag(jnp.swapaxes(bb_im, 1, 2))], axis=1)
    cfull = jnp.concatenate([_group_diag(jnp.swapaxes(c_re, 1, 2)), -_group_diag(jnp.swapaxes(c_im, 1, 2))], axis=0)
    j = jnp.arange(c, dtype=F32)[:, None, None]
    p_re, p_im = _abar_pow(rate[None], ang[None], j)
    cb_re = c_re[None] * p_re[:, :, None, :] - c_im[None] * p_im[:, :, None, :]
    cb_im = c_re[None] * p_im[:, :, None, :] + c_im[None] * p_re[:, :, None, :]
    bt_re, bt_im = jnp.swapaxes(bb_re, 1, 2)[None, :, :, None, :], jnp.swapaxes(bb_im, 1, 2)[None, :, :, None, :]
    kern = jnp.sum(cb_re[:, :, None, :, :] * bt_re - cb_im[:, :, None, :, :] * bt_im, axis=-1)
    kst = jnp.concatenate([_group_diag(kern[c - 1 - s]) for s in range(c)], axis=0)
    n_steps = max(1, (chunks_per_seq - 1).bit_length())
    d = (c * 2.0 ** jnp.arange(n_steps, dtype=F32))[:, None, None]
    s_re, s_im = _abar_pow(rate[None], ang[None], d)
    flat = lambda a: a.reshape(a.shape[0], 1, N_STATE)
    return dict(a_re=ab_re.reshape(1, N_STATE), a_im=ab_im.reshape(1, N_STATE), b=bfull.astype(BF16),
                c=cfull.astype(BF16), kst=kst.astype(BF16), s_re=flat(s_re), s_im=flat(s_im))


def _s5_seq_kernel(ulo_ref, uhi_ref, d_ref, b_ref, c_ref, kst_ref, are_ref, aim_ref, sre_ref, sim_ref,
                   ylo_ref, yhi_ref, hre_ref, him_ref, u2_sc):
    w = SSM_WIDTH
    n = ulo_ref.shape[0] // S5_CHUNK
    a_re = are_ref[...]
    a_im = aim_ref[...]
    token = lambda ref, s: ref[pl.ds(s, n, stride=S5_CHUNK), :]
    h_re = h_im = None
    for s in range(S5_CHUNK):
        ub = jnp.concatenate([token(ulo_ref, s), token(uhi_ref, s)], axis=1).astype(BF16)
        u2_sc[:, s * w:(s + 1) * w] = ub
        x = _dot(ub, b_ref[...])
        x_re, x_im = x[:, :N_STATE], x[:, N_STATE:]
        if s == 0:
            h_re, h_im = x_re, x_im
        else:
            h_re, h_im = a_re * h_re - a_im * h_im + x_re, a_re * h_im + a_im * h_re + x_im
    pos = lax.broadcasted_iota(jnp.int32, (n, 1), 0)
    for k in range(sre_ref.shape[0]):
        d = 1 << k
        keep = pos >= d
        p_re = jnp.where(keep, pltpu.roll(h_re, d, 0), 0.0)
        p_im = jnp.where(keep, pltpu.roll(h_im, d, 0), 0.0)
        s_re = sre_ref[k]
        s_im = sim_ref[k]
        h_re, h_im = h_re + (s_re * p_re - s_im * p_im), h_im + (s_re * p_im + s_im * p_re)
    hre_ref[0] = h_re[n - 1:n]
    him_ref[0] = h_im[n - 1:n]
    keep = pos >= 1
    g_re = jnp.where(keep, pltpu.roll(h_re, 1, 0), 0.0)
    g_im = jnp.where(keep, pltpu.roll(h_im, 1, 0), 0.0)
    for t in range(S5_CHUNK):
        g_re, g_im = a_re * g_re - a_im * g_im, a_re * g_im + a_im * g_re
        g = jnp.concatenate([g_re, g_im], axis=1).astype(BF16)
        y = _dot(u2_sc[:, 0:(t + 1) * w], kst_ref[(S5_CHUNK - 1 - t) * w:, :]) + _dot(g, c_ref[...])
        ylo_ref[pl.ds(t, n, stride=S5_CHUNK), :] = y[:, :LANE] + d_ref[:, :LANE] * token(ulo_ref, t)
        yhi_ref[pl.ds(t, n, stride=S5_CHUNK), :] = y[:, LANE:] + d_ref[:, LANE:] * token(uhi_ref, t)


def _s5_seq_call(u_lo, u_hi, dskip, wts, *, bsz, seq):
    cps = seq // S5_CHUNK
    names = ("b", "c", "kst", "a_re", "a_im", "s_re", "s_im")
    half = pl.BlockSpec((seq, LANE), lambda i: (i, 0))
    state_spec = pl.BlockSpec((1, 1, N_STATE), lambda i: (i, 0, 0))
    y_lo, y_hi, h_re, h_im = pl.pallas_call(
        _s5_seq_kernel,
        grid=(bsz,),
        in_specs=[half, half, _full(dskip)] + [_full(wts[k]) for k in names],
        out_specs=[half, half, state_spec, state_spec],
        out_shape=[jax.ShapeDtypeStruct(u_lo.shape, F32), jax.ShapeDtypeStruct(u_lo.shape, F32),
                   jax.ShapeDtypeStruct((bsz, 1, N_STATE), F32), jax.ShapeDtypeStruct((bsz, 1, N_STATE), F32)],
        scratch_shapes=[pltpu.VMEM((cps, S5_CHUNK * SSM_WIDTH), BF16)],
        compiler_params=_cparams(("arbitrary",)),
        name="s5_seq",
    )(*_raw([u_lo, u_hi, dskip] + [wts[k] for k in names]))
    state = lambda h: h.reshape(bsz, SSM_GROUPS, SSM_STATE)
    return y_lo, y_hi, state(h_re), state(h_im)


def _s5_step_kernel(u_ref, d_ref, h0re_ref, h0im_ref, are_ref, aim_ref, b_ref, c_ref, y_ref, hre_ref, him_ref, *,
                    n_new):
    h_re = h0re_ref[...]
    h_im = h0im_ref[...]
    a_re = are_ref[...]
    a_im = aim_ref[...]
    for t in range(n_new):
        ut = u_ref[t]
        x = _dot(ut.astype(BF16), b_ref[...])
        h_re, h_im = (a_re * h_re - a_im * h_im + x[:, :N_STATE], a_re * h_im + a_im * h_re + x[:, N_STATE:])
        y_ref[t] = _dot(jnp.concatenate([h_re, h_im], axis=1).astype(BF16), c_ref[...]) + d_ref[...] * ut
    hre_ref[...] = h_re
    him_ref[...] = h_im


def _s5_step_call(u, dskip, h0_re, h0_im, wts, *, bsz, n_new):
    u_t = jnp.swapaxes(u.reshape(bsz, n_new, SSM_WIDTH), 0, 1)
    ins = [u_t, dskip, h0_re, h0_im, wts["a_re"], wts["a_im"], wts["b"], wts["c"]]
    outs = [jax.ShapeDtypeStruct(u_t.shape, F32), jax.ShapeDtypeStruct((bsz, N_STATE), F32),
            jax.ShapeDtypeStruct((bsz, N_STATE), F32)]
    y, h_re, h_im = pl.pallas_call(
        functools.partial(_s5_step_kernel, n_new=n_new),
        grid=(1,),
        in_specs=[_full(a) for a in ins],
        out_specs=[_full(o) for o in outs],
        out_shape=outs,
        compiler_params=_cparams(("arbitrary",)),
        name="s5_step",
    )(*_raw(ins))
    y = jnp.swapaxes(y, 0, 1).reshape(bsz * n_new, SSM_WIDTH)
    return (y[:, :LANE], y[:, LANE:], h_re.reshape(bsz, SSM_GROUPS, SSM_STATE),
            h_im.reshape(bsz, SSM_GROUPS, SSM_STATE))


def _merge_kernel(x_ref, sh_ref, sc_ref, gt_ref, g1_ref, wg_ref, yslo_ref, yshi_ref, wglu_ref,
                  gated_ref, wgm_ref, att_ref, wat_ref, wo_ref, o_ref, *, att_feature_major):
    x = x_ref[...]
    h = (_rms(x) * g1_ref[...] * (1.0 + sc_ref[0]) + sh_ref[0]).astype(BF16)
    y_s = jnp.concatenate([yslo_ref[...], yshi_ref[...]], axis=1)
    glu = _dot(_gelu(y_s).astype(BF16), wglu_ref[...])
    d = D_MODEL
    y_att = _dot_tn(att_ref[0], wat_ref[...]) if att_feature_major else _dot(att_ref[...], wat_ref[...])
    merged = jax.nn.sigmoid(_dot(h, wg_ref[:, 0:d])) * (glu[:, :d] * jax.nn.sigmoid(glu[:, d:]))
    merged = merged + jax.nn.sigmoid(_dot(h, wg_ref[:, d:2 * d])) * _dot(gated_ref[...], wgm_ref[...])
    merged = merged + jax.nn.sigmoid(_dot(h, wg_ref[:, 2 * d:3 * d])) * y_att
    o_ref[...] = x + gt_ref[0] * _dot(merged.astype(BF16), wo_ref[...])


def _merge_call(x, ada, g1, wg, ys_lo, ys_hi, wglu, gated, wgm, att, wat, wo, *, tm, tiles_per_batch):
    t = x.shape[0]
    row = lambda n: pl.BlockSpec((tm, n), lambda i: (i, 0))
    fm = att.ndim == 3
    att_spec = (pl.BlockSpec((1, ATT_WIDTH, tm), lambda i: (i // tiles_per_batch, 0, i % tiles_per_batch))
                if fm else row(ATT_WIDTH))
    return pl.pallas_call(
        functools.partial(_merge_kernel, att_feature_major=fm),
        grid=(t // tm,),
        in_specs=[row(D_MODEL), _mod_spec(ada, tiles_per_batch, 0), _mod_spec(ada, tiles_per_batch, 1),
                  _mod_spec(ada, tiles_per_batch, 2), _full(g1), _full(wg), row(LANE), row(LANE),
                  _full(wglu), row(GMLP_WIDTH), _full(wgm), att_spec, _full(wat), _full(wo)],
        out_specs=row(D_MODEL),
        out_shape=jax.ShapeDtypeStruct(x.shape, F32),
        compiler_params=_cparams(("arbitrary",)),
        name="merge_out",
    )(*_raw([x, ada, ada, ada, g1, wg, ys_lo, ys_hi, wglu, gated, wgm, att, wat, wo]))


def _mlp_kernel(x_ref, sh_ref, sc_ref, gt_ref, g2_ref, wup_ref, wdn_ref, gfin_ref, o_ref, *, final):
    x = x_ref[...]
    h = (_rms(x) * g2_ref[...] * (1.0 + sc_ref[0]) + sh_ref[0]).astype(BF16)
    blk = D_FF // FF_SPLIT
    acc = None
    for c in range(FF_SPLIT):
        a = jnp.maximum(_dot(h, wup_ref[:, c * blk:(c + 1) * blk]), 0.0)
        part = _dot((a * a).astype(BF16), wdn_ref[c * blk:(c + 1) * blk, :])
        acc = part if acc is None else acc + part
    out = x + gt_ref[0] * acc
    o_ref[...] = _rms(out) * gfin_ref[...] if final else out


def _mlp_call(x, ada, g2, wup, wdn, gfin, *, tm, tiles_per_batch, final):
    t = x.shape[0]
    row = lambda n: pl.BlockSpec((tm, n), lambda i: (i, 0))
    return pl.pallas_call(
        functools.partial(_mlp_kernel, final=final),
        grid=(t // tm,),
        in_specs=[row(D_MODEL), _mod_spec(ada, tiles_per_batch, 3), _mod_spec(ada, tiles_per_batch, 4),
                  _mod_spec(ada, tiles_per_batch, 5), _full(g2), _full(wup), _full(wdn), _full(gfin)],
        out_specs=row(D_MODEL),
        out_shape=jax.ShapeDtypeStruct(x.shape, F32),
        compiler_params=_cparams(("arbitrary",)),
        name="mlp",
    )(*_raw([x, ada, ada, ada, g2, wup, wdn, gfin]))


def _mix_weights(w_s, b_s, seq, rows):
    cl = min(seq, CHUNK)
    depth = w_s.shape[0]
    w = jnp.tril(w_s[:, :, :cl, :cl])
    reps = rows // cl
    eye = jnp.eye(reps, dtype=w.dtype)
    wmix = jnp.einsum("lgts,ab->lgatbs", w, eye).reshape(depth, GMLP_GROUPS, rows, rows).astype(BF16)
    bias = jnp.tile(jnp.swapaxes(b_s[:, :, :cl], 1, 2), (1, reps, 1))
    bmix = jnp.repeat(bias, GMLP_GROUP_CH, axis=2)
    return wmix, bmix


def _aug_constants():
    hd = jnp.arange(ATT_HEADS)
    pk = jnp.zeros((3, LANE, LANE), F32)
    pq = jnp.zeros((LANE, 4 * ATT_HEADS), F32)
    ones_k = jnp.zeros((1, LANE), F32)
    ones_q = jnp.zeros((LANE, 1), F32)
    for r in range(3):
        pk = pk.at[r, hd, hd * AUG_ROWS + 3 + r].set(-1.0)
        pq = pq.at[hd * AUG_ROWS + r, r * ATT_HEADS + hd].set(1.0)
        ones_k = ones_k.at[0, hd * AUG_ROWS + r].set(1.0)
        ones_q = ones_q.at[hd * AUG_ROWS + 3 + r, 0].set(1.0)
    return pk.astype(BF16), ones_k, pq.astype(BF16), ones_q


def _cumsum_tri(rows, seq):
    r = jnp.arange(rows)
    return ((r[:, None] >= r[None, :]) & (r[:, None] // seq == r[None, :] // seq)).astype(BF16)


def kernel(x_prompt, x_sample, c_prompt, c_sample, cache_k, cache_v, cache_logf, state_ssm_re, state_ssm_im, page_table, w_ada, b_ada, g_norm1, w_in, b_f, ssm_a_re, ssm_a_im, ssm_log_dt, ssm_b_re, ssm_b_im, ssm_c_re, ssm_c_im, ssm_d, w_glu, g_gv, w_s, b_s, w_gmlp_out, w_att_out, w_o, g_norm2, w_up, w_down, g_final):
    depth = w_in.shape[0]
    bp, lp, d = x_prompt.shape
    bs, ls, _ = x_sample.shape
    tp, ts = bp * lp, bs * ls
    tm_p = TM_PROMPT
    tiles_pb = lp // tm_p

    ada = _ada_call(jnp.concatenate([c_prompt, c_sample], axis=0), w_ada, b_ada)
    ada_p = ada[:, :bp].reshape(depth, bp, 1, 6 * d)
    ada_s = jnp.repeat(ada[:, bp:], ls, axis=1).reshape(depth, 1, ts, 6 * d)
    n_pool, page = cache_k.shape[1], cache_k.shape[2]
    feature_major = lambda c: jnp.transpose(c, (0, 1, 3, 4, 2)).reshape(depth, n_pool, ATT_WIDTH, page)
    cache_kt = feature_major(cache_k)
    cache_vt = feature_major(cache_v)
    cache_lf_t = jnp.swapaxes(cache_logf, 2, 3)

    pad_f = lambda w: jnp.pad(w, ((0, 0), (0, 0), (0, LANE - ATT_HEADS)))
    w_qkv, w_f, w_sgg = w_in[:, :, Q_OFF:F_OFF], w_in[:, :, F_OFF:S_OFF], w_in[:, :, S_OFF:G_OFF]
    stacked = dict(
        w_row_p=jnp.concatenate([w_in[:, :, K_OFF:V_OFF], w_sgg, pad_f(w_f)], axis=2).astype(BF16),
        w_t_p=jnp.pad(jnp.swapaxes(jnp.concatenate([w_qkv, w_f], axis=2), 1, 2),
                      ((0, 0), (0, BF16_ROWS - ATT_HEADS), (0, 0))).astype(BF16),
        w_row_s=jnp.concatenate([w_qkv, w_sgg, pad_f(w_f)], axis=2).astype(BF16),
        w_gate=w_in[:, :, G_OFF:].astype(BF16),
        bf_row=jnp.pad(b_f, ((0, 0), (0, LANE - ATT_HEADS))).reshape(depth, 1, LANE),
        bf_col=b_f.reshape(depth, ATT_HEADS, 1),
        g1=g_norm1.reshape(depth, 1, d), g2=g_norm2.reshape(depth, 1, d),
        ggv=g_gv.reshape(depth, 1, GMLP_WIDTH), dskip=ssm_d.reshape(depth, 1, SSM_WIDTH),
        wglu=w_glu.astype(BF16), wgm=w_gmlp_out.astype(BF16), wat=w_att_out.astype(BF16), wo=w_o.astype(BF16),
        wup=w_up.astype(BF16), wdn=w_down.astype(BF16),
        h0_re=state_ssm_re.reshape(depth, bs, N_STATE), h0_im=state_ssm_im.reshape(depth, bs, N_STATE),
        ada_p=ada_p, ada_s=ada_s)
    stacked["wmix_p"], stacked["bmix_p"] = _mix_weights(w_s, b_s, lp, CHUNK)
    stacked["wmix_s"], stacked["bmix_s"] = _mix_weights(w_s, b_s, ls, ts)
    s5_stacked = jax.vmap(functools.partial(_s5_weights, chunks_per_seq=lp // S5_CHUNK))(
        ssm_a_re, ssm_a_im, ssm_log_dt, ssm_b_re, ssm_b_im, ssm_c_re, ssm_c_im)
    tri_p = _cumsum_tri(tm_p, tm_p)
    tri_s = _cumsum_tri(ts, ls)
    gfin = g_final.reshape(1, d)
    aug = _aug_constants()

    xp = x_prompt.reshape(tp, d)
    xs = x_sample.reshape(ts, d)
    outs = {k: [] for k in ("kp", "vp", "lfp", "hrp", "hip", "gvp", "ks", "vs", "lfs", "hrs", "his", "gvs")}
    token_major = lambda a: jnp.transpose(a.reshape(bp, ATT_HEADS, HEAD_DIM, lp), (0, 3, 1, 2))
    for l in range(depth):
        w = {k: _Layer(v, l) for k, v in stacked.items()}
        s5w = {k: _Layer(v, l) for k, v in s5_stacked.items()}

        qt, ktf, vtf, vtb, kb, lft, kaug, qaug, u_lo, u_hi, gated, gv = _in_proj_p_call(
            xp, w["ada_p"], w["g1"], w["w_row_p"], w["w_t_p"], w["bf_row"], w["bf_col"], w["ggv"], w["wmix_p"],
            w["bmix_p"], tri_p, tri_p.T, aug, bsz=bp, seq=lp, tm=tm_p)
        att_t = _flash_call(kb, qt, vtb, kaug, qaug, bsz=bp, seq=lp, tq=tm_p)
        ys_lo, ys_hi, hr, hi = _s5_seq_call(u_lo, u_hi, w["dskip"], s5w, bsz=bp, seq=lp)
        xp = _merge_call(xp, w["ada_p"], w["g1"], w["w_gate"], ys_lo, ys_hi, w["wglu"], gated, w["wgm"], att_t,
                         w["wat"], w["wo"], tm=tm_p, tiles_per_batch=tiles_pb)
        xp = _mlp_call(xp, w["ada_p"], w["g2"], w["wup"], w["wdn"], gfin, tm=tm_p, tiles_per_batch=tiles_pb,
                       final=l == depth - 1)
        outs["kp"].append(token_major(ktf))
        outs["vp"].append(token_major(vtf))
        outs["lfp"].append(jnp.swapaxes(lft, 1, 2))
        outs["hrp"].append(hr)
        outs["hip"].append(hi)
        outs["gvp"].append(gv)

        q, kf, kb, vf, vb, lf, cum, u, gated, gv = _in_proj_s_call(
            xs, w["ada_s"], w["g1"], w["w_row_s"], w["bf_row"], w["ggv"], w["wmix_s"], w["bmix_s"], tri_s)
        cum_t = jnp.swapaxes(cum.reshape(bs, ls, ATT_HEADS), 1, 2)
        att = _decode_call(page_table, q, kb, vb, cum, cum_t, cache_kt, cache_vt, cache_lf_t,
                           layer=l, bsz=bs, n_new=ls)
        ys_lo, ys_hi, hr, hi = _s5_step_call(u, w["dskip"], w["h0_re"], w["h0_im"], s5w, bsz=bs, n_new=ls)
        xs = _merge_call(xs, w["ada_s"], w["g1"], w["w_gate"], ys_lo, ys_hi, w["wglu"], gated, w["wgm"], att,
                         w["wat"], w["wo"], tm=ts, tiles_per_batch=1)
        xs = _mlp_call(xs, w["ada_s"], w["g2"], w["wup"], w["wdn"], gfin, tm=ts, tiles_per_batch=1,
                       final=l == depth - 1)
        outs["ks"].append(kf.reshape(bs, ls, ATT_HEADS, HEAD_DIM))
        outs["vs"].append(vf.reshape(bs, ls, ATT_HEADS, HEAD_DIM))
        outs["lfs"].append(lf.reshape(bs, ls, ATT_HEADS))
        outs["hrs"].append(hr)
        outs["his"].append(hi)
        outs["gvs"].append(gv.reshape(bs, ls, GMLP_WIDTH))

    st = lambda k: jnp.stack(outs[k])
    return (xp.reshape(bp, lp, d), xs.reshape(bs, ls, d), st("kp"), st("vp"), st("lfp"), st("hrp"), st("hip"), st("gvp"),
            st("ks"), st("vs"), st("lfs"), st("hrs"), st("his"), st("gvs"))
```

```python
import functools
import math

import jax
import jax.numpy as jnp
from jax import lax
from jax.experimental import pallas as pl
from jax.experimental.pallas import tpu as pltpu

F32 = jnp.float32
BF16 = jnp.bfloat16

D_MODEL = 1024
ATT_HEADS = 8
HEAD_DIM = 64
ATT_WIDTH = ATT_HEADS * HEAD_DIM
SSM_GROUPS = 16
SSM_GROUP_CH = 16
SSM_WIDTH = SSM_GROUPS * SSM_GROUP_CH
SSM_STATE = 64
N_STATE = SSM_GROUPS * SSM_STATE
GMLP_GROUPS = 4
GMLP_GROUP_CH = 64
GMLP_WIDTH = GMLP_GROUPS * GMLP_GROUP_CH
CHUNK = 128
N_BRANCH = 3
D_FF = 4 * D_MODEL
EPS = 1e-6

Q_OFF = 0
K_OFF = Q_OFF + ATT_WIDTH
V_OFF = K_OFF + ATT_WIDTH
F_OFF = V_OFF + ATT_WIDTH
S_OFF = F_OFF + ATT_HEADS
GU_OFF = S_OFF + SSM_WIDTH
GV_OFF = GU_OFF + GMLP_WIDTH
G_OFF = GV_OFF + GMLP_WIDTH

LOG2E = math.log2(math.e)
LANE = 128
BF16_ROWS = 16
S5_CHUNK = 16
TM_PROMPT = 512
FLASH_HEADS = 8
DEC_PAGES = 8
DEC_SLOTS = 4
FF_SPLIT = 4
VMEM_LIMIT = 56 * 1024 * 1024


def _cparams(sem):
    return pltpu.CompilerParams(dimension_semantics=sem, vmem_limit_bytes=VMEM_LIMIT)


def _dot(a, b):
    return jnp.dot(a, b, preferred_element_type=F32)


def _dot_nt(a, b):
    return lax.dot_general(a, b, (((1,), (1,)), ((), ())), preferred_element_type=F32)


def _dot_tn(a, b):
    return lax.dot_general(a, b, (((0,), (0,)), ((), ())), preferred_element_type=F32)


def _split3(x):
    hi = x.astype(BF16)
    r1 = x - hi.astype(F32)
    mid = r1.astype(BF16)
    lo = (r1 - mid.astype(F32)).astype(BF16)
    return hi, mid, lo


def _gelu(x):
    return 0.5 * x * (1.0 + jnp.tanh(math.sqrt(2.0 / math.pi) * (x + 0.044715 * (x * x * x))))


def _log_sigmoid(x):
    return -(jnp.maximum(-x, 0.0) + jnp.log1p(jnp.exp(-jnp.abs(x))))


def _rms(x):
    return x * lax.rsqrt(jnp.mean(x * x, axis=-1, keepdims=True) + EPS)


class _Layer:
    def __init__(self, stacked, index):
        self.stacked, self.index = stacked, index

    @property
    def shape(self):
        return self.stacked.shape[1:]

    @property
    def ndim(self):
        return self.stacked.ndim - 1


def _raw(args):
    return [a.stacked if isinstance(a, _Layer) else a for a in args]


def _full(arr):
    if isinstance(arr, _Layer):
        l, shape, zeros = arr.index, arr.shape, (0,) * arr.ndim
        return pl.BlockSpec((None,) + shape, lambda *_: (l,) + zeros)
    return pl.BlockSpec(arr.shape, lambda *_: (0,) * arr.ndim)


def _ada_kernel(c_ref, w_ref, b_ref, o_ref):
    c = c_ref[...]
    a = (c * jax.nn.sigmoid(c)).astype(BF16)
    o_ref[0] = _dot(a, w_ref[0].astype(BF16)) + b_ref[0]


def _ada_call(c_all, w_ada, b_ada):
    depth, d, n = w_ada.shape
    r = c_all.shape[0]
    tn = 1536
    return pl.pallas_call(
        _ada_kernel,
        grid=(depth, n // tn),
        in_specs=[pl.BlockSpec((r, d), lambda l, j: (0, 0)),
                  pl.BlockSpec((1, d, tn), lambda l, j: (l, 0, j)),
                  pl.BlockSpec((1, 1, tn), lambda l, j: (l, 0, j))],
        out_specs=pl.BlockSpec((1, r, tn), lambda l, j: (l, 0, j)),
        out_shape=jax.ShapeDtypeStruct((depth, r, n), F32),
        compiler_params=_cparams(("arbitrary", "arbitrary")),
        name="ada",
    )(c_all, w_ada, b_ada.reshape(depth, 1, n))


def _mod_spec(ada, tiles_per_batch, j):
    l, r = ada.index, ada.shape[1]
    return pl.BlockSpec((None, 1, r, D_MODEL), lambda i: (l, i // tiles_per_batch, 0, j))


def _gmlp_gate(gu, gv, wmix_ref, bmix_ref, gated_ref, mix_rows):
    lane = lax.broadcasted_iota(jnp.int32, (1, GMLP_WIDTH), 1)
    for c in range(gu.shape[0] // mix_rows):
        rows = slice(c * mix_rows, (c + 1) * mix_rows)
        gvc = gv[rows]
        s = bmix_ref[...]
        for g in range(GMLP_GROUPS):
            in_g = (lane >= g * GMLP_GROUP_CH) & (lane < (g + 1) * GMLP_GROUP_CH)
            s = s + _dot(wmix_ref[g], jnp.where(in_g, gvc, 0.0).astype(BF16))
        gated_ref[rows, :] = (gu[rows] * s).astype(BF16)


W_ROW_COLS = SSM_WIDTH + 2 * GMLP_WIDTH
W_T_ROWS = 3 * ATT_WIDTH + BF16_ROWS


def _in_proj_p_kernel(x_ref, sh_ref, sc_ref, g1_ref, wr_ref, wt_ref, bfc_ref, ggv_ref, wmix_ref, bmix_ref,
                      triu_ref, pk_ref, onesk_ref, pq_ref, onesq_ref,
                      qt_ref, ktf_ref, ktb_ref, vtf_ref, vtb_ref, lft_ref, kaug_ref, qaug_ref, ulo_ref, uhi_ref,
                      gated_ref, gv_ref, carry_c, *, tiles_per_batch, mix_rows):
    i = pl.program_id(0)
    tm = x_ref.shape[0]
    a = ATT_WIDTH
    h = (_rms(x_ref[...]) * g1_ref[...] * (1.0 + sc_ref[0]) + sh_ref[0]).astype(BF16)

    @pl.when(i % tiles_per_batch == 0)
    def _():
        carry_c[...] = jnp.zeros_like(carry_c)

    zt = _dot_nt(wt_ref[...], h)
    qt_ref[0] = (zt[0:a] * (LOG2E * HEAD_DIM ** -0.5)).astype(BF16)
    kt = zt[a:2 * a]
    ktf_ref[0] = kt
    ktb_ref[0] = kt.astype(BF16)
    vt = zt[2 * a:3 * a]
    vtf_ref[0] = vt
    vtb_ref[0] = vt.astype(BF16)
    lft = _log_sigmoid(zt[3 * a:3 * a + ATT_HEADS] + bfc_ref[...])
    lft_ref[0] = lft
    hi, mid, lo = _split3(lft)
    triu = triu_ref[...]
    cumt = (_dot(hi, triu) + _dot(mid, triu)) + _dot(lo, triu) + carry_c[...]
    carry_c[...] = cumt[:, tm - 1:tm]
    parts = [t.astype(F32) for t in _split3(cumt * LOG2E)] + [jnp.zeros_like(cumt)]
    parts = jnp.concatenate(parts, axis=0).astype(BF16)
    qaug_ref[0] = (_dot(pq_ref[...], parts) + onesq_ref[...]).astype(BF16)
    kaug_ref[0] = (_dot(pk_ref[...], parts) + onesk_ref[...]).astype(BF16)

    o = 0
    u = _dot(h, wr_ref[:, o:o + SSM_WIDTH])
    ulo_ref[...] = u[:, :LANE]
    uhi_ref[...] = u[:, LANE:]
    o += SSM_WIDTH
    gu = _gelu(_dot(h, wr_ref[:, o:o + GMLP_WIDTH]))
    o += GMLP_WIDTH
    gv = _rms(_gelu(_dot(h, wr_ref[:, o:o + GMLP_WIDTH]))) * ggv_ref[...]
    gv_ref[0] = gv[tm - mix_rows:tm]
    o += GMLP_WIDTH

    _gmlp_gate(gu, gv, wmix_ref, bmix_ref, gated_ref, mix_rows)


def _in_proj_p_call(x, ada, g1, wr, wt, bfc, ggv, wmix, bmix, triu, aug, *, bsz, seq, tm):
    t = x.shape[0]
    tpb = seq // tm
    row = lambda n: pl.BlockSpec((tm, n), lambda i: (i, 0))
    fm = lambda n: pl.BlockSpec((1, n, tm), lambda i: (i // tpb, 0, i % tpb))
    outs = [(fm(ATT_WIDTH), (bsz, ATT_WIDTH, seq), BF16),
            (fm(ATT_WIDTH), (bsz, ATT_WIDTH, seq), F32),
            (fm(ATT_WIDTH), (bsz, ATT_WIDTH, seq), BF16),
            (fm(ATT_WIDTH), (bsz, ATT_WIDTH, seq), F32),
            (fm(ATT_WIDTH), (bsz, ATT_WIDTH, seq), BF16),
            (fm(ATT_HEADS), (bsz, ATT_HEADS, seq), F32),
            (fm(LANE), (bsz, LANE, seq), BF16),
            (fm(LANE), (bsz, LANE, seq), BF16),
            (row(LANE), (t, LANE), F32),
            (row(LANE), (t, LANE), F32),
            (row(GMLP_WIDTH), (t, GMLP_WIDTH), BF16),
            (pl.BlockSpec((1, CHUNK, GMLP_WIDTH), lambda i: (i // tpb, 0, 0)),
             (bsz, CHUNK, GMLP_WIDTH), F32)]
    assert seq % CHUNK == 0 and wmix.shape[1] == CHUNK
    return pl.pallas_call(
        functools.partial(_in_proj_p_kernel, tiles_per_batch=tpb, mix_rows=wmix.shape[1]),
        grid=(t // tm,),
        in_specs=[row(D_MODEL), _mod_spec(ada, tpb, 0), _mod_spec(ada, tpb, 1), _full(g1), _full(wr), _full(wt),
                  _full(bfc), _full(ggv), _full(wmix), _full(bmix), _full(triu)] + [_full(a) for a in aug],
        out_specs=[o[0] for o in outs],
        out_shape=[jax.ShapeDtypeStruct(o[1], o[2]) for o in outs],
        scratch_shapes=[pltpu.VMEM((ATT_HEADS, 1), F32)],
        compiler_params=_cparams(("arbitrary",)),
        name="in_proj_p",
    )(*_raw([x, ada, ada, g1, wr, wt, bfc, ggv, wmix, bmix, triu, *aug]))


W_S_COLS = 3 * ATT_WIDTH + SSM_WIDTH + 2 * GMLP_WIDTH + LANE


def _in_proj_s_kernel(x_ref, sh_ref, sc_ref, g1_ref, w_ref, bf_ref, ggv_ref, wmix_ref, bmix_ref, tri_ref,
                      q_ref, kf_ref, kb_ref, vf_ref, vb_ref, lf_ref, cum_ref, u_ref, gated_ref, gv_ref, *, mix_rows):
    h = (_rms(x_ref[...]) * g1_ref[...] * (1.0 + sc_ref[0]) + sh_ref[0]).astype(BF16)
    a = ATT_WIDTH
    q_ref[...] = (_dot(h, w_ref[:, 0:a]) * (HEAD_DIM ** -0.5)).astype(BF16)
    k = _dot(h, w_ref[:, a:2 * a])
    kf_ref[...] = k
    kb_ref[...] = k.astype(BF16)
    v = _dot(h, w_ref[:, 2 * a:3 * a])
    vf_ref[...] = v
    vb_ref[...] = v.astype(BF16)
    o = 3 * a
    u_ref[...] = _dot(h, w_ref[:, o:o + SSM_WIDTH])
    o += SSM_WIDTH
    gu = _gelu(_dot(h, w_ref[:, o:o + GMLP_WIDTH]))
    o += GMLP_WIDTH
    gv = _rms(_gelu(_dot(h, w_ref[:, o:o + GMLP_WIDTH]))) * ggv_ref[...]
    gv_ref[...] = gv
    o += GMLP_WIDTH
    lf = _log_sigmoid(_dot(h, w_ref[:, o:o + LANE]) + bf_ref[...])
    lf_ref[...] = lf[:, :ATT_HEADS]
    hi, mid, lo = _split3(lf)
    tri = tri_ref[...]
    cum = (_dot(tri, hi) + _dot(tri, mid)) + _dot(tri, lo)
    cum_ref[...] = cum[:, :ATT_HEADS]
    _gmlp_gate(gu, gv, wmix_ref, bmix_ref, gated_ref, mix_rows)


def _in_proj_s_call(x, ada, g1, w, bf, ggv, wmix, bmix, tri):
    t = x.shape[0]
    row = lambda n: pl.BlockSpec((t, n), lambda i: (0, 0))
    outs = [(ATT_WIDTH, BF16), (ATT_WIDTH, F32), (ATT_WIDTH, BF16), (ATT_WIDTH, F32), (ATT_WIDTH, BF16),
            (ATT_HEADS, F32), (ATT_HEADS, F32), (SSM_WIDTH, F32), (GMLP_WIDTH, BF16), (GMLP_WIDTH, F32)]
    return pl.pallas_call(
        functools.partial(_in_proj_s_kernel, mix_rows=wmix.shape[1]),
        grid=(1,),
        in_specs=[row(D_MODEL), _mod_spec(ada, 1, 0), _mod_spec(ada, 1, 1),
                  _full(g1), _full(w), _full(bf), _full(ggv), _full(wmix), _full(bmix), _full(tri)],
        out_specs=[row(n) for n, _ in outs],
        out_shape=[jax.ShapeDtypeStruct((t, n), dt) for n, dt in outs],
        compiler_params=_cparams(("arbitrary",)),
        name="in_proj_s",
    )(*_raw([x, ada, ada, g1, w, bf, ggv, wmix, bmix, tri]))


AUG_ROWS = 16
VT_ROWS = HEAD_DIM + BF16_ROWS


def _flash_kernel(qi_tab, ki_tab, k_ref, qt_ref, vt_ref, kaug_ref, qaug_ref, o_ref, m_sc, acc_sc):
    hg = pl.program_id(1)
    s_idx = pl.program_id(2)
    qi = qi_tab[s_idx]
    ki = ki_tab[s_idx]
    tk = k_ref.shape[2]
    tq = qt_ref.shape[2]
    pair = 2 * HEAD_DIM

    @pl.when(ki == 0)
    def _():
        m_sc[...] = jnp.full_like(m_sc, -jnp.inf)
        acc_sc[...] = jnp.zeros_like(acc_sc)

    def step(masked):
        feat = lax.broadcasted_iota(jnp.int32, (pair, 1), 0)
        kaug = kaug_ref[0]
        qaug = qaug_ref[0]
        ones_tile = (lax.broadcasted_iota(jnp.int32, (BF16_ROWS, tk), 0) == 0).astype(BF16)
        sts = []
        for h in range(FLASH_HEADS):
            hp, hh = divmod(h, 2)
            head = hg * FLASH_HEADS + h
            k = jnp.concatenate([k_ref[0, hp * pair:(hp + 1) * pair, :], kaug], axis=0)
            qt = qt_ref[0, hp * pair:(hp + 1) * pair, :]
            own = (feat >= hh * HEAD_DIM) & (feat < (hh + 1) * HEAD_DIM)
            own_aug = (feat >= head * AUG_ROWS) & (feat < (head + 1) * AUG_ROWS)
            qfull = jnp.concatenate([jnp.where(own, qt, jnp.zeros_like(qt)),
                                     jnp.where(own_aug, qaug, jnp.zeros_like(qaug))], axis=0)
            st = _dot_tn(k, qfull)
            if masked:
                kj = lax.broadcasted_iota(jnp.int32, (tk, tq), 0)
                qc = lax.broadcasted_iota(jnp.int32, (tk, tq), 1)
                st = jnp.where(kj <= qc, st, -jnp.inf)
            sts.append(st)
        for h in range(FLASH_HEADS):
            st = sts[h]
            m_prev = m_sc[h]
            m_new = jnp.maximum(m_prev, jnp.max(st, axis=0, keepdims=True))
            alpha = jnp.exp2(m_prev - m_new)
            p = jnp.exp2(st - m_new).astype(BF16)
            vt = jnp.concatenate([vt_ref[0, h * HEAD_DIM:(h + 1) * HEAD_DIM, :], ones_tile], axis=0)
            acc_sc[h] = alpha * acc_sc[h] + _dot(vt, p)
            m_sc[h] = m_new

    @pl.when(ki < qi)
    def _():
        step(False)

    @pl.when(ki == qi)
    def _():
        step(True)
        for h in range(FLASH_HEADS):
            acc = acc_sc[h]
            o_ref[0, h * HEAD_DIM:(h + 1) * HEAD_DIM, :] = (acc[:HEAD_DIM] / acc[HEAD_DIM:HEAD_DIM + 1]).astype(o_ref.dtype)


def _flash_call(k, qt, vt, kaug, qaug, *, bsz, seq, tq):
    nq = seq // tq
    tri = [(a, b) for a in range(nq) for b in range(a + 1)]
    qi_tab = jnp.array([a for a, _ in tri], jnp.int32)
    ki_tab = jnp.array([b for _, b in tri], jnp.int32)
    width = FLASH_HEADS * HEAD_DIM
    q_spec = pl.BlockSpec((1, width, tq), lambda b, hg, s, qt_, kt_: (b, hg, qt_[s]))
    return pl.pallas_call(
        _flash_kernel,
        grid_spec=pltpu.PrefetchScalarGridSpec(
            num_scalar_prefetch=2,
            grid=(bsz, ATT_HEADS // FLASH_HEADS, len(tri)),
            in_specs=[pl.BlockSpec((1, width, tq), lambda b, hg, s, qt_, kt_: (b, hg, kt_[s])),
                      q_spec,
                      pl.BlockSpec((1, width, tq), lambda b, hg, s, qt_, kt_: (b, hg, kt_[s])),
                      pl.BlockSpec((1, LANE, tq), lambda b, hg, s, qt_, kt_: (b, 0, kt_[s])),
                      pl.BlockSpec((1, LANE, tq), lambda b, hg, s, qt_, kt_: (b, 0, qt_[s]))],
            out_specs=q_spec,
            scratch_shapes=[pltpu.VMEM((FLASH_HEADS, 1, tq), F32), pltpu.VMEM((FLASH_HEADS, VT_ROWS, tq), F32)]),
        out_shape=jax.ShapeDtypeStruct(qt.shape, BF16),
        compiler_params=_cparams(("arbitrary", "arbitrary", "arbitrary")),
        name="flash",
    )(qi_tab, ki_tab, k, qt, vt, kaug, qaug)


def _decode_kernel(pt_ref, q_ref, kn_ref, vn_ref, cn_ref, cnt_ref, hmask_ref, qmask_ref, nmask_ref, tri_ref,
                   ck_hbm, cv_hbm, clf_hbm, o_ref, kbuf, vbuf, lbuf, sem, *, layer, n_pages):
    b = pl.program_id(0)
    nb = pl.num_programs(0)
    page = lbuf.shape[3]
    n_chunks = n_pages // DEC_PAGES
    n_q = q_ref.shape[0]
    rows = n_q * ATT_HEADS

    def copies(bb, chunk, slot):
        out = []
        for j in range(DEC_PAGES):
            pg = pt_ref[bb, chunk * DEC_PAGES + j]
            cols = pl.ds(j * page, page)
            out.append(pltpu.make_async_copy(ck_hbm.at[layer, pg], kbuf.at[slot, :, cols], sem.at[slot, 0]))
            out.append(pltpu.make_async_copy(cv_hbm.at[layer, pg], vbuf.at[slot, :, cols], sem.at[slot, 1]))
            out.append(pltpu.make_async_copy(clf_hbm.at[layer, pg], lbuf.at[slot, j], sem.at[slot, 2]))
        return out

    def chunk_of(step):
        return n_chunks - 1 - step

    ahead = DEC_SLOTS - 1

    @pl.when(b == 0)
    def _():
        for step in range(ahead):
            for cp in copies(b, chunk_of(step), step % DEC_SLOTS):
                cp.start()

    q = q_ref[...].astype(F32)
    qbd = jnp.broadcast_to(q[:, None, :], (n_q, ATT_HEADS, ATT_WIDTH)).reshape(rows, ATT_WIDTH)
    qbd = jnp.where(hmask_ref[...] > 0, qbd, 0.0).astype(BF16)
    cn = cn_ref[...]
    cn_rows = jnp.broadcast_to(cn[:, None, :], (n_q, ATT_HEADS, ATT_HEADS)).reshape(rows, ATT_HEADS)
    fq = jnp.sum(jnp.where(qmask_ref[...] > 0, cn_rows, 0.0), axis=-1, keepdims=True)

    s = _dot_nt(qbd, kn_ref[...])
    fk_new = jnp.concatenate([cnt_ref[0]] * n_q, axis=0)
    s = jnp.where(nmask_ref[...] > 0, s + fq - fk_new, -jnp.inf)
    m = jnp.max(s, axis=-1, keepdims=True)
    p = jnp.exp(s - m)
    l = jnp.sum(p, axis=-1, keepdims=True)
    acc = _dot(p.astype(BF16), vn_ref[...])
    run = jnp.zeros((ATT_HEADS, 1), F32)

    for step in range(n_chunks):
        slot = step % DEC_SLOTS
        for cp in copies(b, chunk_of(step), slot):
            cp.wait()
        nxt = step + ahead
        if nxt < n_chunks:
            for cp in copies(b, chunk_of(nxt), nxt % DEC_SLOTS):
                cp.start()
        else:
            @pl.when(b + 1 < nb)
            def _(nxt=nxt):
                for cp in copies(b + 1, chunk_of(nxt - n_chunks), nxt % DEC_SLOTS):
                    cp.start()

        kc = kbuf[slot].astype(BF16)
        vc = vbuf[slot].astype(BF16)
        lf = lbuf[slot].reshape(DEC_PAGES * ATT_HEADS, page)
        hi, mid, lo = _split3(lf)
        tri = tri_ref[...]
        suf = (_dot(hi, tri) + _dot(mid, tri)) + _dot(lo, tri)
        tot = jnp.sum(lf, axis=-1, keepdims=True)
        bias = []
        for j in reversed(range(DEC_PAGES)):
            pr = slice(j * ATT_HEADS, (j + 1) * ATT_HEADS)
            bias.append(suf[pr] + run)
            run = run + tot[pr]
        bias = jnp.concatenate(bias[::-1], axis=1)
        s = _dot(qbd, kc) + fq + jnp.concatenate([bias] * n_q, axis=0)
        m_new = jnp.maximum(m, jnp.max(s, axis=-1, keepdims=True))
        alpha = jnp.exp(m - m_new)
        p = jnp.exp(s - m_new)
        l = alpha * l + jnp.sum(p, axis=-1, keepdims=True)
        acc = alpha * acc + _dot_nt(p.astype(BF16), vc)
        m = m_new

    out = jnp.where(hmask_ref[...] > 0, acc / l, 0.0)
    o_ref[...] = jnp.sum(out.reshape(n_q, ATT_HEADS, ATT_WIDTH), axis=1).astype(o_ref.dtype)


def _decode_call(page_table, q, k_new, v_new, cn, cn_t, cache_kt, cache_vt, cache_lf_t, *, layer, bsz, n_new):
    n_pages = page_table.shape[1]
    assert n_pages % (DEC_PAGES * DEC_SLOTS) == 0
    page = cache_kt.shape[3]
    rows = n_new * ATT_HEADS
    r = jnp.arange(rows)
    hmask = (r[:, None] % ATT_HEADS == jnp.arange(ATT_WIDTH)[None, :] // HEAD_DIM).astype(F32)
    qmask = (r[:, None] % ATT_HEADS == jnp.arange(ATT_HEADS)[None, :]).astype(F32)
    nmask = (r[:, None] // ATT_HEADS >= jnp.arange(n_new)[None, :]).astype(F32)
    tri = (jnp.arange(page)[:, None] > jnp.arange(page)[None, :]).astype(BF16)
    tok = lambda n: pl.BlockSpec((n_new, n), lambda b, pt: (b, 0))
    hbm = pl.BlockSpec(memory_space=pl.ANY)
    return pl.pallas_call(
        functools.partial(_decode_kernel, layer=layer, n_pages=n_pages),
        grid_spec=pltpu.PrefetchScalarGridSpec(
            num_scalar_prefetch=1,
            grid=(bsz,),
            in_specs=[tok(ATT_WIDTH), tok(ATT_WIDTH), tok(ATT_WIDTH), tok(ATT_HEADS),
                      pl.BlockSpec((1, ATT_HEADS, n_new), lambda b, pt: (b, 0, 0)),
                      _full(hmask), _full(qmask), _full(nmask), _full(tri), hbm, hbm, hbm],
            out_specs=tok(ATT_WIDTH),
            scratch_shapes=[pltpu.VMEM((DEC_SLOTS, ATT_WIDTH, DEC_PAGES * page), F32),
                            pltpu.VMEM((DEC_SLOTS, ATT_WIDTH, DEC_PAGES * page), F32),
                            pltpu.VMEM((DEC_SLOTS, DEC_PAGES, ATT_HEADS, page), F32),
                            pltpu.SemaphoreType.DMA((DEC_SLOTS, 3))]),
        out_shape=jax.ShapeDtypeStruct(q.shape, BF16),
        compiler_params=_cparams(("arbitrary",)),
        name="decode",
    )(page_table, q, k_new, v_new, cn, cn_t, hmask, qmask, nmask, tri, cache_kt, cache_vt, cache_lf_t)


def _s5_discretise(a_re, a_im, log_dt, b_re, b_im):
    dt = jnp.exp(log_dt)[:, None]
    rate = a_re * dt
    ang = a_im * dt
    mag = jnp.exp(rate)
    ab_re = mag * jnp.cos(ang)
    ab_im = mag * jnp.sin(ang)
    den = a_re * a_re + a_im * a_im
    z_re = ((ab_re - 1.0) * a_re + ab_im * a_im) / den
    z_im = (ab_im * a_re - (ab_re - 1.0) * a_im) / den
    bb_re = z_re[..., None] * b_re - z_im[..., None] * b_im
    bb_im = z_re[..., None] * b_im + z_im[..., None] * b_re
    return rate, ang, ab_re, ab_im, bb_re, bb_im


def _abar_pow(rate, ang, j):
    mag = jnp.exp(rate * j)
    return mag * jnp.cos(ang * j), mag * jnp.sin(ang * j)


def _group_diag(m):
    g, r, c = m.shape
    return jnp.einsum("grc,gh->grhc", m, jnp.eye(g, dtype=m.dtype)).reshape(g * r, g * c)


def _s5_weights(a_re, a_im, log_dt, b_re, b_im, c_re, c_im, chunks_per_seq):
    rate, ang, ab_re, ab_im, bb_re, bb_im = _s5_discretise(a_re, a_im, log_dt, b_re, b_im)
    c = S5_CHUNK
    bfull = jnp.concatenate([_group_diag(jnp.swapaxes(bb_re, 1, 2)), _group_diag(jnp.swapaxes(bb_im, 1, 2))], axis=1)
    cfull = jnp.concatenate([_group_diag(jnp.swapaxes(c_re, 1, 2)), -_group_diag(jnp.swapaxes(c_im, 1, 2))], axis=0)
    j = jnp.arange(c, dtype=F32)[:, None, None]
    p_re, p_im = _abar_pow(rate[None], ang[None], j)
    cb_re = c_re[None] * p_re[:, :, None, :] - c_im[None] * p_im[:, :, None, :]
    cb_im = c_re[None] * p_im[:, :, None, :] + c_im[None] * p_re[:, :, None, :]
    bt_re, bt_im = jnp.swapaxes(bb_re, 1, 2)[None, :, :, None, :], jnp.swapaxes(bb_im, 1, 2)[None, :, :, None, :]
    kern = jnp.sum(cb_re[:, :, None, :, :] * bt_re - cb_im[:, :, None, :, :] * bt_im, axis=-1)
    kst = jnp.concatenate([_group_diag(kern[c - 1 - s]) for s in range(c)], axis=0)
    n_steps = max(1, (chunks_per_seq - 1).bit_length())
    d = (c * 2.0 ** jnp.arange(n_steps, dtype=F32))[:, None, None]
    s_re, s_im = _abar_pow(rate[None], ang[None], d)
    flat = lambda a: a.reshape(a.shape[0], 1, N_STATE)
    return dict(a_re=ab_re.reshape(1, N_STATE), a_im=ab_im.reshape(1, N_STATE), b=bfull.astype(BF16),
                c=cfull.astype(BF16), kst=kst.astype(BF16), s_re=flat(s_re), s_im=flat(s_im))


def _s5_seq_kernel(ulo_ref, uhi_ref, d_ref, b_ref, c_ref, kst_ref, are_ref, aim_ref, sre_ref, sim_ref,
                   ylo_ref, yhi_ref, hre_ref, him_ref, u2_sc):
    w = SSM_WIDTH
    n = ulo_ref.shape[0] // S5_CHUNK
    a_re = are_ref[...]
    a_im = aim_ref[...]
    token = lambda ref, s: ref[pl.ds(s, n, stride=S5_CHUNK), :]
    h_re = h_im = None
    for s in range(S5_CHUNK):
        ub = jnp.concatenate([token(ulo_ref, s), token(uhi_ref, s)], axis=1).astype(BF16)
        u2_sc[:, s * w:(s + 1) * w] = ub
        x = _dot(ub, b_ref[...])
        x_re, x_im = x[:, :N_STATE], x[:, N_STATE:]
        if s == 0:
            h_re, h_im = x_re, x_im
        else:
            h_re, h_im = a_re * h_re - a_im * h_im + x_re, a_re * h_im + a_im * h_re + x_im
    pos = lax.broadcasted_iota(jnp.int32, (n, 1), 0)
    for k in range(sre_ref.shape[0]):
        d = 1 << k
        keep = pos >= d
        p_re = jnp.where(keep, pltpu.roll(h_re, d, 0), 0.0)
        p_im = jnp.where(keep, pltpu.roll(h_im, d, 0), 0.0)
        s_re = sre_ref[k]
        s_im = sim_ref[k]
        h_re, h_im = h_re + (s_re * p_re - s_im * p_im), h_im + (s_re * p_im + s_im * p_re)
    hre_ref[0] = h_re[n - 1:n]
    him_ref[0] = h_im[n - 1:n]
    keep = pos >= 1
    g_re = jnp.where(keep, pltpu.roll(h_re, 1, 0), 0.0)
    g_im = jnp.where(keep, pltpu.roll(h_im, 1, 0), 0.0)
    for t in range(S5_CHUNK):
        g_re, g_im = a_re * g_re - a_im * g_im, a_re * g_im + a_im * g_re
        g = jnp.concatenate([g_re, g_im], axis=1).astype(BF16)
        y = _dot(u2_sc[:, 0:(t + 1) * w], kst_ref[(S5_CHUNK - 1 - t) * w:, :]) + _dot(g, c_ref[...])
        ylo_ref[pl.ds(t, n, stride=S5_CHUNK), :] = y[:, :LANE] + d_ref[:, :LANE] * token(ulo_ref, t)
        yhi_ref[pl.ds(t, n, stride=S5_CHUNK), :] = y[:, LANE:] + d_ref[:, LANE:] * token(uhi_ref, t)


def _s5_seq_call(u_lo, u_hi, dskip, wts, *, bsz, seq):
    cps = seq // S5_CHUNK
    names = ("b", "c", "kst", "a_re", "a_im", "s_re", "s_im")
    half = pl.BlockSpec((seq, LANE), lambda i: (i, 0))
    state_spec = pl.BlockSpec((1, 1, N_STATE), lambda i: (i, 0, 0))
    y_lo, y_hi, h_re, h_im = pl.pallas_call(
        _s5_seq_kernel,
        grid=(bsz,),
        in_specs=[half, half, _full(dskip)] + [_full(wts[k]) for k in names],
        out_specs=[half, half, state_spec, state_spec],
        out_shape=[jax.ShapeDtypeStruct(u_lo.shape, F32), jax.ShapeDtypeStruct(u_lo.shape, F32),
                   jax.ShapeDtypeStruct((bsz, 1, N_STATE), F32), jax.ShapeDtypeStruct((bsz, 1, N_STATE), F32)],
        scratch_shapes=[pltpu.VMEM((cps, S5_CHUNK * SSM_WIDTH), BF16)],
        compiler_params=_cparams(("arbitrary",)),
        name="s5_seq",
    )(*_raw([u_lo, u_hi, dskip] + [wts[k] for k in names]))
    state = lambda h: h.reshape(bsz, SSM_GROUPS, SSM_STATE)
    return y_lo, y_hi, state(h_re), state(h_im)


def _s5_step_kernel(u_ref, d_ref, h0re_ref, h0im_ref, are_ref, aim_ref, b_ref, c_ref, y_ref, hre_ref, him_ref, *,
                    n_new):
    h_re = h0re_ref[...]
    h_im = h0im_ref[...]
    a_re = are_ref[...]
    a_im = aim_ref[...]
    for t in range(n_new):
        ut = u_ref[t]
        x = _dot(ut.astype(BF16), b_ref[...])
        h_re, h_im = (a_re * h_re - a_im * h_im + x[:, :N_STATE], a_re * h_im + a_im * h_re + x[:, N_STATE:])
        y_ref[t] = _dot(jnp.concatenate([h_re, h_im], axis=1).astype(BF16), c_ref[...]) + d_ref[...] * ut
    hre_ref[...] = h_re
    him_ref[...] = h_im


def _s5_step_call(u, dskip, h0_re, h0_im, wts, *, bsz, n_new):
    u_t = jnp.swapaxes(u.reshape(bsz, n_new, SSM_WIDTH), 0, 1)
    ins = [u_t, dskip, h0_re, h0_im, wts["a_re"], wts["a_im"], wts["b"], wts["c"]]
    outs = [jax.ShapeDtypeStruct(u_t.shape, F32), jax.ShapeDtypeStruct((bsz, N_STATE), F32),
            jax.ShapeDtypeStruct((bsz, N_STATE), F32)]
    y, h_re, h_im = pl.pallas_call(
        functools.partial(_s5_step_kernel, n_new=n_new),
        grid=(1,),
        in_specs=[_full(a) for a in ins],
        out_specs=[_full(o) for o in outs],
        out_shape=outs,
        compiler_params=_cparams(("arbitrary",)),
        name="s5_step",
    )(*_raw(ins))
    y = jnp.swapaxes(y, 0, 1).reshape(bsz * n_new, SSM_WIDTH)
    return (y[:, :LANE], y[:, LANE:], h_re.reshape(bsz, SSM_GROUPS, SSM_STATE),
            h_im.reshape(bsz, SSM_GROUPS, SSM_STATE))


def _merge_kernel(x_ref, sh_ref, sc_ref, gt_ref, g1_ref, wg_ref, yslo_ref, yshi_ref, wglu_ref,
                  gated_ref, wgm_ref, att_ref, wat_ref, wo_ref, o_ref, *, att_feature_major):
    x = x_ref[...]
    h = (_rms(x) * g1_ref[...] * (1.0 + sc_ref[0]) + sh_ref[0]).astype(BF16)
    y_s = jnp.concatenate([yslo_ref[...], yshi_ref[...]], axis=1)
    glu = _dot(_gelu(y_s).astype(BF16), wglu_ref[...])
    d = D_MODEL
    y_att = _dot_tn(att_ref[0], wat_ref[...]) if att_feature_major else _dot(att_ref[...], wat_ref[...])
    merged = jax.nn.sigmoid(_dot(h, wg_ref[:, 0:d])) * (glu[:, :d] * jax.nn.sigmoid(glu[:, d:]))
    merged = merged + jax.nn.sigmoid(_dot(h, wg_ref[:, d:2 * d])) * _dot(gated_ref[...], wgm_ref[...])
    merged = merged + jax.nn.sigmoid(_dot(h, wg_ref[:, 2 * d:3 * d])) * y_att
    o_ref[...] = x + gt_ref[0] * _dot(merged.astype(BF16), wo_ref[...])


def _merge_call(x, ada, g1, wg, ys_lo, ys_hi, wglu, gated, wgm, att, wat, wo, *, tm, tiles_per_batch):
    t = x.shape[0]
    row = lambda n: pl.BlockSpec((tm, n), lambda i: (i, 0))
    fm = att.ndim == 3
    att_spec = (pl.BlockSpec((1, ATT_WIDTH, tm), lambda i: (i // tiles_per_batch, 0, i % tiles_per_batch))
                if fm else row(ATT_WIDTH))
    return pl.pallas_call(
        functools.partial(_merge_kernel, att_feature_major=fm),
        grid=(t // tm,),
        in_specs=[row(D_MODEL), _mod_spec(ada, tiles_per_batch, 0), _mod_spec(ada, tiles_per_batch, 1),
                  _mod_spec(ada, tiles_per_batch, 2), _full(g1), _full(wg), row(LANE), row(LANE),
                  _full(wglu), row(GMLP_WIDTH), _full(wgm), att_spec, _full(wat), _full(wo)],
        out_specs=row(D_MODEL),
        out_shape=jax.ShapeDtypeStruct(x.shape, F32),
        compiler_params=_cparams(("arbitrary",)),
        name="merge_out",
    )(*_raw([x, ada, ada, ada, g1, wg, ys_lo, ys_hi, wglu, gated, wgm, att, wat, wo]))


def _mlp_kernel(x_ref, sh_ref, sc_ref, gt_ref, g2_ref, wup_ref, wdn_ref, gfin_ref, o_ref, *, final):
    x = x_ref[...]
    h = (_rms(x) * g2_ref[...] * (1.0 + sc_ref[0]) + sh_ref[0]).astype(BF16)
    blk = D_FF // FF_SPLIT
    acc = None
    for c in range(FF_SPLIT):
        a = jnp.maximum(_dot(h, wup_ref[:, c * blk:(c + 1) * blk]), 0.0)
        part = _dot((a * a).astype(BF16), wdn_ref[c * blk:(c + 1) * blk, :])
        acc = part if acc is None else acc + part
    out = x + gt_ref[0] * acc
    o_ref[...] = _rms(out) * gfin_ref[...] if final else out


def _mlp_call(x, ada, g2, wup, wdn, gfin, *, tm, tiles_per_batch, final):
    t = x.shape[0]
    row = lambda n: pl.BlockSpec((tm, n), lambda i: (i, 0))
    return pl.pallas_call(
        functools.partial(_mlp_kernel, final=final),
        grid=(t // tm,),
        in_specs=[row(D_MODEL), _mod_spec(ada, tiles_per_batch, 3), _mod_spec(ada, tiles_per_batch, 4),
                  _mod_spec(ada, tiles_per_batch, 5), _full(g2), _full(wup), _full(wdn), _full(gfin)],
        out_specs=row(D_MODEL),
        out_shape=jax.ShapeDtypeStruct(x.shape, F32),
        compiler_params=_cparams(("arbitrary",)),
        name="mlp",
    )(*_raw([x, ada, ada, ada, g2, wup, wdn, gfin]))


def _mix_weights(w_s, b_s, seq, rows):
    cl = min(seq, CHUNK)
    w = jnp.tril(w_s[:, :, :cl, :cl])
    r = jnp.arange(rows)
    pos = r % cl
    tiled = jnp.take(jnp.take(w, pos, axis=2), pos, axis=3)
    wmix = jnp.where(r[:, None] // cl == r[None, :] // cl, tiled, 0.0).astype(BF16)
    bias = jnp.take(jnp.swapaxes(b_s[:, :, :cl], 1, 2), pos, axis=1)
    bmix = jnp.repeat(bias, GMLP_GROUP_CH, axis=2)
    return wmix, bmix


def _aug_constants():
    hd = jnp.arange(ATT_HEADS)
    pk = jnp.zeros((LANE, 4 * ATT_HEADS), F32)
    pq = jnp.zeros((LANE, 4 * ATT_HEADS), F32)
    ones_k = jnp.zeros((LANE, 1), F32)
    ones_q = jnp.zeros((LANE, 1), F32)
    for r in range(3):
        pk = pk.at[hd * AUG_ROWS + 3 + r, r * ATT_HEADS + hd].set(-1.0)
        pq = pq.at[hd * AUG_ROWS + r, r * ATT_HEADS + hd].set(1.0)
        ones_k = ones_k.at[hd * AUG_ROWS + r, 0].set(1.0)
        ones_q = ones_q.at[hd * AUG_ROWS + 3 + r, 0].set(1.0)
    return pk.astype(BF16), ones_k, pq.astype(BF16), ones_q


def _cumsum_tri(rows, seq):
    r = jnp.arange(rows)
    return ((r[:, None] >= r[None, :]) & (r[:, None] // seq == r[None, :] // seq)).astype(BF16)


def kernel(x_prompt, x_sample, c_prompt, c_sample, cache_k, cache_v, cache_logf, state_ssm_re, state_ssm_im, page_table, w_ada, b_ada, g_norm1, w_in, b_f, ssm_a_re, ssm_a_im, ssm_log_dt, ssm_b_re, ssm_b_im, ssm_c_re, ssm_c_im, ssm_d, w_glu, g_gv, w_s, b_s, w_gmlp_out, w_att_out, w_o, g_norm2, w_up, w_down, g_final):
    depth = w_in.shape[0]
    bp, lp, d = x_prompt.shape
    bs, ls, _ = x_sample.shape
    tp, ts = bp * lp, bs * ls
    tm_p = TM_PROMPT
    tiles_pb = lp // tm_p

    ada = _ada_call(jnp.concatenate([c_prompt, c_sample], axis=0), w_ada, b_ada)
    ada_p = ada[:, :bp].reshape(depth, bp, 1, 6 * d)
    ada_s = jnp.repeat(ada[:, bp:], ls, axis=1).reshape(depth, 1, ts, 6 * d)
    n_pool, page = cache_k.shape[1], cache_k.shape[2]
    feature_major = lambda c: jnp.transpose(c, (0, 1, 3, 4, 2)).reshape(depth, n_pool, ATT_WIDTH, page)
    cache_kt = feature_major(cache_k)
    cache_vt = feature_major(cache_v)
    cache_lf_t = jnp.swapaxes(cache_logf, 2, 3)

    pad_f = lambda w: jnp.pad(w, ((0, 0), (0, 0), (0, LANE - ATT_HEADS)))
    w_qkv, w_f, w_sgg = w_in[:, :, Q_OFF:F_OFF], w_in[:, :, F_OFF:S_OFF], w_in[:, :, S_OFF:G_OFF]
    stacked = dict(
        w_row_p=w_sgg.astype(BF16),
        w_t_p=jnp.pad(jnp.swapaxes(jnp.concatenate([w_qkv, w_f], axis=2), 1, 2),
                      ((0, 0), (0, BF16_ROWS - ATT_HEADS), (0, 0))).astype(BF16),
        w_row_s=jnp.concatenate([w_qkv, w_sgg, pad_f(w_f)], axis=2).astype(BF16),
        w_gate=w_in[:, :, G_OFF:].astype(BF16),
        bf_row=jnp.pad(b_f, ((0, 0), (0, LANE - ATT_HEADS))).reshape(depth, 1, LANE),
        bf_col=b_f.reshape(depth, ATT_HEADS, 1),
        g1=g_norm1.reshape(depth, 1, d), g2=g_norm2.reshape(depth, 1, d),
        ggv=g_gv.reshape(depth, 1, GMLP_WIDTH), dskip=ssm_d.reshape(depth, 1, SSM_WIDTH),
        wglu=w_glu.astype(BF16), wgm=w_gmlp_out.astype(BF16), wat=w_att_out.astype(BF16), wo=w_o.astype(BF16),
        wup=w_up.astype(BF16), wdn=w_down.astype(BF16),
        h0_re=state_ssm_re.reshape(depth, bs, N_STATE), h0_im=state_ssm_im.reshape(depth, bs, N_STATE),
        ada_p=ada_p, ada_s=ada_s)
    stacked["wmix_p"], stacked["bmix_p"] = _mix_weights(w_s, b_s, lp, CHUNK)
    stacked["wmix_s"], stacked["bmix_s"] = _mix_weights(w_s, b_s, ls, ts)
    s5_stacked = jax.vmap(functools.partial(_s5_weights, chunks_per_seq=lp // S5_CHUNK))(
        ssm_a_re, ssm_a_im, ssm_log_dt, ssm_b_re, ssm_b_im, ssm_c_re, ssm_c_im)
    tri_p = _cumsum_tri(tm_p, tm_p)
    tri_s = _cumsum_tri(ts, ls)
    gfin = g_final.reshape(1, d)
    aug = _aug_constants()

    xp = x_prompt.reshape(tp, d)
    xs = x_sample.reshape(ts, d)
    outs = {k: [] for k in ("kp", "vp", "lfp", "hrp", "hip", "gvp", "ks", "vs", "lfs", "hrs", "his", "gvs")}
    token_major = lambda a: jnp.transpose(a.reshape(bp, ATT_HEADS, HEAD_DIM, lp), (0, 3, 1, 2))
    for l in range(depth):
        w = {k: _Layer(v, l) for k, v in stacked.items()}
        s5w = {k: _Layer(v, l) for k, v in s5_stacked.items()}

        qt, ktf, ktb, vtf, vtb, lft, kaug, qaug, u_lo, u_hi, gated, gv = _in_proj_p_call(
            xp, w["ada_p"], w["g1"], w["w_row_p"], w["w_t_p"], w["bf_col"], w["ggv"], w["wmix_p"], w["bmix_p"],
            tri_p.T, aug, bsz=bp, seq=lp, tm=tm_p)
        att_t = _flash_call(ktb, qt, vtb, kaug, qaug, bsz=bp, seq=lp, tq=tm_p)
        ys_lo, ys_hi, hr, hi = _s5_seq_call(u_lo, u_hi, w["dskip"], s5w, bsz=bp, seq=lp)
        xp = _merge_call(xp, w["ada_p"], w["g1"], w["w_gate"], ys_lo, ys_hi, w["wglu"], gated, w["wgm"], att_t,
                         w["wat"], w["wo"], tm=tm_p, tiles_per_batch=tiles_pb)
        xp = _mlp_call(xp, w["ada_p"], w["g2"], w["wup"], w["wdn"], gfin, tm=tm_p, tiles_per_batch=tiles_pb,
                       final=l == depth - 1)
        outs["kp"].append(token_major(ktf))
        outs["vp"].append(token_major(vtf))
        outs["lfp"].append(jnp.swapaxes(lft, 1, 2))
        outs["hrp"].append(hr)
        outs["hip"].append(hi)
        outs["gvp"].append(gv)

        q, kf, kb, vf, vb, lf, cum, u, gated, gv = _in_proj_s_call(
            xs, w["ada_s"], w["g1"], w["w_row_s"], w["bf_row"], w["ggv"], w["wmix_s"], w["bmix_s"], tri_s)
        cum_t = jnp.swapaxes(cum.reshape(bs, ls, ATT_HEADS), 1, 2)
        att = _decode_call(page_table, q, kb, vb, cum, cum_t, cache_kt, cache_vt, cache_lf_t,
                           layer=l, bsz=bs, n_new=ls)
        ys_lo, ys_hi, hr, hi = _s5_step_call(u, w["dskip"], w["h0_re"], w["h0_im"], s5w, bsz=bs, n_new=ls)
        xs = _merge_call(xs, w["ada_s"], w["g1"], w["w_gate"], ys_lo, ys_hi, w["wglu"], gated, w["wgm"], att,
                         w["wat"], w["wo"], tm=ts, tiles_per_batch=1)
        xs = _mlp_call(xs, w["ada_s"], w["g2"], w["wup"], w["wdn"], gfin, tm=ts, tiles_per_batch=1,
                       final=l == depth - 1)
        outs["ks"].append(kf.reshape(bs, ls, ATT_HEADS, HEAD_DIM))
        outs["vs"].append(vf.reshape(bs, ls, ATT_HEADS, HEAD_DIM))
        outs["lfs"].append(lf.reshape(bs, ls, ATT_HEADS))
        outs["hrs"].append(hr)
        outs["his"].append(hi)
        outs["gvs"].append(gv.reshape(bs, ls, GMLP_WIDTH))

    st = lambda k: jnp.stack(outs[k])
    return (xp.reshape(bp, lp, d), xs.reshape(bs, ls, d), st("kp"), st("vp"), st("lfp"), st("hrp"), st("hip"), st("gvp"),
            st("ks"), st("vs"), st("lfs"), st("hrs"), st("his"), st("gvs"))
```

```python
import functools
import math

import jax
import jax.numpy as jnp
from jax import lax
from jax.experimental import pallas as pl
from jax.experimental.pallas import tpu as pltpu

F32 = jnp.float32
BF16 = jnp.bfloat16

D_MODEL = 1024
ATT_HEADS = 8
HEAD_DIM = 64
ATT_WIDTH = ATT_HEADS * HEAD_DIM
SSM_GROUPS = 16
SSM_GROUP_CH = 16
SSM_WIDTH = SSM_GROUPS * SSM_GROUP_CH
SSM_STATE = 64
N_STATE = SSM_GROUPS * SSM_STATE
GMLP_GROUPS = 4
GMLP_GROUP_CH = 64
GMLP_WIDTH = GMLP_GROUPS * GMLP_GROUP_CH
CHUNK = 128
N_BRANCH = 3
D_FF = 4 * D_MODEL
EPS = 1e-6

Q_OFF = 0
K_OFF = Q_OFF + ATT_WIDTH
V_OFF = K_OFF + ATT_WIDTH
F_OFF = V_OFF + ATT_WIDTH
S_OFF = F_OFF + ATT_HEADS
GU_OFF = S_OFF + SSM_WIDTH
GV_OFF = GU_OFF + GMLP_WIDTH
G_OFF = GV_OFF + GMLP_WIDTH

LOG2E = math.log2(math.e)
LANE = 128
BF16_ROWS = 16
S5_CHUNK = 16
TM_PROMPT = 512
FLASH_HEADS = 8
DEC_PAGES = 8
DEC_SLOTS = 4
FF_SPLIT = 4
VMEM_LIMIT = 56 * 1024 * 1024


def _cparams(sem):
    return pltpu.CompilerParams(dimension_semantics=sem, vmem_limit_bytes=VMEM_LIMIT)


def _dot(a, b):
    return jnp.dot(a, b, preferred_element_type=F32)


def _dot_nt(a, b):
    return lax.dot_general(a, b, (((1,), (1,)), ((), ())), preferred_element_type=F32)


def _dot_tn(a, b):
    return lax.dot_general(a, b, (((0,), (0,)), ((), ())), preferred_element_type=F32)


def _split3(x):
    hi = x.astype(BF16)
    r1 = x - hi.astype(F32)
    mid = r1.astype(BF16)
    lo = (r1 - mid.astype(F32)).astype(BF16)
    return hi, mid, lo


def _gelu(x):
    return 0.5 * x * (1.0 + jnp.tanh(math.sqrt(2.0 / math.pi) * (x + 0.044715 * (x * x * x))))


def _log_sigmoid(x):
    return -(jnp.maximum(-x, 0.0) + jnp.log1p(jnp.exp(-jnp.abs(x))))


def _rms(x):
    return x * lax.rsqrt(jnp.mean(x * x, axis=-1, keepdims=True) + EPS)


class _Layer:
    def __init__(self, stacked, index):
        self.stacked, self.index = stacked, index

    @property
    def shape(self):
        return self.stacked.shape[1:]

    @property
    def ndim(self):
        return self.stacked.ndim - 1


def _raw(args):
    return [a.stacked if isinstance(a, _Layer) else a for a in args]


def _full(arr):
    if isinstance(arr, _Layer):
        l, shape, zeros = arr.index, arr.shape, (0,) * arr.ndim
        return pl.BlockSpec((None,) + shape, lambda *_: (l,) + zeros)
    return pl.BlockSpec(arr.shape, lambda *_: (0,) * arr.ndim)


def _ada_kernel(c_ref, w_ref, b_ref, o_ref):
    c = c_ref[...]
    a = (c * jax.nn.sigmoid(c)).astype(BF16)
    o_ref[0] = _dot(a, w_ref[0].astype(BF16)) + b_ref[0]


def _ada_call(c_all, w_ada, b_ada):
    depth, d, n = w_ada.shape
    r = c_all.shape[0]
    tn = 1536
    return pl.pallas_call(
        _ada_kernel,
        grid=(depth, n // tn),
        in_specs=[pl.BlockSpec((r, d), lambda l, j: (0, 0)),
                  pl.BlockSpec((1, d, tn), lambda l, j: (l, 0, j)),
                  pl.BlockSpec((1, 1, tn), lambda l, j: (l, 0, j))],
        out_specs=pl.BlockSpec((1, r, tn), lambda l, j: (l, 0, j)),
        out_shape=jax.ShapeDtypeStruct((depth, r, n), F32),
        compiler_params=_cparams(("arbitrary", "arbitrary")),
        name="ada",
    )(c_all, w_ada, b_ada.reshape(depth, 1, n))


def _mod_spec(ada, tiles_per_batch, j):
    l, r = ada.index, ada.shape[1]
    return pl.BlockSpec((None, 1, r, D_MODEL), lambda i: (l, i // tiles_per_batch, 0, j))


def _gmlp_gate(gu, gv, wmix_ref, bmix_ref, gated_ref, mix_rows):
    lane = lax.broadcasted_iota(jnp.int32, (1, GMLP_WIDTH), 1)
    for c in range(gu.shape[0] // mix_rows):
        rows = slice(c * mix_rows, (c + 1) * mix_rows)
        gvc = gv[rows]
        s = bmix_ref[...]
        for g in range(GMLP_GROUPS):
            in_g = (lane >= g * GMLP_GROUP_CH) & (lane < (g + 1) * GMLP_GROUP_CH)
            s = s + _dot(wmix_ref[g], jnp.where(in_g, gvc, 0.0).astype(BF16))
        gated_ref[rows, :] = (gu[rows] * s).astype(BF16)


W_ROW_COLS = SSM_WIDTH + 2 * GMLP_WIDTH
W_T_ROWS = 3 * ATT_WIDTH + BF16_ROWS


def _in_proj_p_kernel(x_ref, sh_ref, sc_ref, g1_ref, wr_ref, wt_ref, bfc_ref, ggv_ref, wmix_ref, bmix_ref,
                      triu_ref, pk_ref, onesk_ref, pq_ref, onesq_ref,
                      qt_ref, ktf_ref, ktb_ref, vtf_ref, vtb_ref, lft_ref, kaug_ref, qaug_ref, ulo_ref, uhi_ref,
                      gated_ref, gv_ref, carry_c, *, tiles_per_batch, mix_rows):
    i = pl.program_id(0)
    tm = x_ref.shape[0]
    a = ATT_WIDTH
    h = (_rms(x_ref[...]) * g1_ref[...] * (1.0 + sc_ref[0]) + sh_ref[0]).astype(BF16)

    @pl.when(i % tiles_per_batch == 0)
    def _():
        carry_c[...] = jnp.zeros_like(carry_c)

    zt = _dot_nt(wt_ref[...], h)
    qt_ref[0] = (zt[0:a] * (LOG2E * HEAD_DIM ** -0.5)).astype(BF16)
    kt = zt[a:2 * a]
    ktf_ref[0] = kt
    ktb_ref[0] = kt.astype(BF16)
    vt = zt[2 * a:3 * a]
    vtf_ref[0] = vt
    vtb_ref[0] = vt.astype(BF16)
    lft = _log_sigmoid(zt[3 * a:3 * a + ATT_HEADS] + bfc_ref[...])
    lft_ref[0] = lft
    hi, mid, lo = _split3(lft)
    triu = triu_ref[...]
    cumt = (_dot(hi, triu) + _dot(mid, triu)) + _dot(lo, triu) + carry_c[...]
    carry_c[...] = cumt[:, tm - 1:tm]
    parts = [t.astype(F32) for t in _split3(cumt * LOG2E)] + [jnp.zeros_like(cumt)]
    parts = jnp.concatenate(parts, axis=0).astype(BF16)
    qaug_ref[0] = (_dot(pq_ref[...], parts) + onesq_ref[...]).astype(BF16)
    kaug_ref[0] = (_dot(pk_ref[...], parts) + onesk_ref[...]).astype(BF16)

    o = 0
    u = _dot(h, wr_ref[:, o:o + SSM_WIDTH])
    ulo_ref[...] = u[:, :LANE]
    uhi_ref[...] = u[:, LANE:]
    o += SSM_WIDTH
    gu = _gelu(_dot(h, wr_ref[:, o:o + GMLP_WIDTH]))
    o += GMLP_WIDTH
    gv = _rms(_gelu(_dot(h, wr_ref[:, o:o + GMLP_WIDTH]))) * ggv_ref[...]
    gv_ref[0] = gv[tm - mix_rows:tm]
    o += GMLP_WIDTH

    _gmlp_gate(gu, gv, wmix_ref, bmix_ref, gated_ref, mix_rows)


def _in_proj_p_call(x, ada, g1, wr, wt, bfc, ggv, wmix, bmix, triu, aug, *, bsz, seq, tm):
    t = x.shape[0]
    tpb = seq // tm
    row = lambda n: pl.BlockSpec((tm, n), lambda i: (i, 0))
    fm = lambda n: pl.BlockSpec((1, n, tm), lambda i: (i // tpb, 0, i % tpb))
    outs = [(fm(ATT_WIDTH), (bsz, ATT_WIDTH, seq), BF16),
            (fm(ATT_WIDTH), (bsz, ATT_WIDTH, seq), F32),
            (fm(ATT_WIDTH), (bsz, ATT_WIDTH, seq), BF16),
            (fm(ATT_WIDTH), (bsz, ATT_WIDTH, seq), F32),
            (fm(ATT_WIDTH), (bsz, ATT_WIDTH, seq), BF16),
            (fm(ATT_HEADS), (bsz, ATT_HEADS, seq), F32),
            (fm(LANE), (bsz, LANE, seq), BF16),
            (fm(LANE), (bsz, LANE, seq), BF16),
            (row(LANE), (t, LANE), F32),
            (row(LANE), (t, LANE), F32),
            (row(GMLP_WIDTH), (t, GMLP_WIDTH), BF16),
            (pl.BlockSpec((1, CHUNK, GMLP_WIDTH), lambda i: (i // tpb, 0, 0)),
             (bsz, CHUNK, GMLP_WIDTH), F32)]
    assert seq % CHUNK == 0 and wmix.shape[1] == CHUNK
    return pl.pallas_call(
        functools.partial(_in_proj_p_kernel, tiles_per_batch=tpb, mix_rows=wmix.shape[1]),
        grid=(t // tm,),
        in_specs=[row(D_MODEL), _mod_spec(ada, tpb, 0), _mod_spec(ada, tpb, 1), _full(g1), _full(wr), _full(wt),
                  _full(bfc), _full(ggv), _full(wmix), _full(bmix), _full(triu)] + [_full(a) for a in aug],
        out_specs=[o[0] for o in outs],
        out_shape=[jax.ShapeDtypeStruct(o[1], o[2]) for o in outs],
        scratch_shapes=[pltpu.VMEM((ATT_HEADS, 1), F32)],
        compiler_params=_cparams(("arbitrary",)),
        name="in_proj_p",
    )(*_raw([x, ada, ada, g1, wr, wt, bfc, ggv, wmix, bmix, triu, *aug]))


W_S_COLS = 3 * ATT_WIDTH + SSM_WIDTH + 2 * GMLP_WIDTH + LANE


def _in_proj_s_kernel(x_ref, sh_ref, sc_ref, g1_ref, w_ref, bf_ref, ggv_ref, wmix_ref, bmix_ref, tri_ref,
                      q_ref, kf_ref, kb_ref, vf_ref, vb_ref, lf_ref, cum_ref, u_ref, gated_ref, gv_ref, *, mix_rows):
    h = (_rms(x_ref[...]) * g1_ref[...] * (1.0 + sc_ref[0]) + sh_ref[0]).astype(BF16)
    a = ATT_WIDTH
    q_ref[...] = (_dot(h, w_ref[:, 0:a]) * (HEAD_DIM ** -0.5)).astype(BF16)
    k = _dot(h, w_ref[:, a:2 * a])
    kf_ref[...] = k
    kb_ref[...] = k.astype(BF16)
    v = _dot(h, w_ref[:, 2 * a:3 * a])
    vf_ref[...] = v
    vb_ref[...] = v.astype(BF16)
    o = 3 * a
    u_ref[...] = _dot(h, w_ref[:, o:o + SSM_WIDTH])
    o += SSM_WIDTH
    gu = _gelu(_dot(h, w_ref[:, o:o + GMLP_WIDTH]))
    o += GMLP_WIDTH
    gv = _rms(_gelu(_dot(h, w_ref[:, o:o + GMLP_WIDTH]))) * ggv_ref[...]
    gv_ref[...] = gv
    o += GMLP_WIDTH
    lf = _log_sigmoid(_dot(h, w_ref[:, o:o + LANE]) + bf_ref[...])
    lf_ref[...] = lf[:, :ATT_HEADS]
    hi, mid, lo = _split3(lf)
    tri = tri_ref[...]
    cum = (_dot(tri, hi) + _dot(tri, mid)) + _dot(tri, lo)
    cum_ref[...] = cum[:, :ATT_HEADS]
    _gmlp_gate(gu, gv, wmix_ref, bmix_ref, gated_ref, mix_rows)


def _in_proj_s_call(x, ada, g1, w, bf, ggv, wmix, bmix, tri):
    t = x.shape[0]
    row = lambda n: pl.BlockSpec((t, n), lambda i: (0, 0))
    outs = [(ATT_WIDTH, BF16), (ATT_WIDTH, F32), (ATT_WIDTH, BF16), (ATT_WIDTH, F32), (ATT_WIDTH, BF16),
            (ATT_HEADS, F32), (ATT_HEADS, F32), (SSM_WIDTH, F32), (GMLP_WIDTH, BF16), (GMLP_WIDTH, F32)]
    return pl.pallas_call(
        functools.partial(_in_proj_s_kernel, mix_rows=wmix.shape[1]),
        grid=(1,),
        in_specs=[row(D_MODEL), _mod_spec(ada, 1, 0), _mod_spec(ada, 1, 1),
                  _full(g1), _full(w), _full(bf), _full(ggv), _full(wmix), _full(bmix), _full(tri)],
        out_specs=[row(n) for n, _ in outs],
        out_shape=[jax.ShapeDtypeStruct((t, n), dt) for n, dt in outs],
        compiler_params=_cparams(("arbitrary",)),
        name="in_proj_s",
    )(*_raw([x, ada, ada, g1, w, bf, ggv, wmix, bmix, tri]))


AUG_ROWS = 16
VT_ROWS = HEAD_DIM + BF16_ROWS


def _flash_kernel(qi_tab, ki_tab, k_ref, qt_ref, vt_ref, kaug_ref, qaug_ref, o_ref, m_sc, acc_sc):
    hg = pl.program_id(1)
    s_idx = pl.program_id(2)
    qi = qi_tab[s_idx]
    ki = ki_tab[s_idx]
    tk = k_ref.shape[2]
    tq = qt_ref.shape[2]
    pair = 2 * HEAD_DIM

    @pl.when(ki == 0)
    def _():
        m_sc[...] = jnp.full_like(m_sc, -jnp.inf)
        acc_sc[...] = jnp.zeros_like(acc_sc)

    def step(masked):
        feat = lax.broadcasted_iota(jnp.int32, (pair, 1), 0)
        kaug = kaug_ref[0]
        qaug = qaug_ref[0]
        ones_tile = (lax.broadcasted_iota(jnp.int32, (BF16_ROWS, tk), 0) == 0).astype(BF16)
        sts = []
        for h in range(FLASH_HEADS):
            hp, hh = divmod(h, 2)
            head = hg * FLASH_HEADS + h
            k = jnp.concatenate([k_ref[0, hp * pair:(hp + 1) * pair, :], kaug], axis=0)
            qt = qt_ref[0, hp * pair:(hp + 1) * pair, :]
            own = (feat >= hh * HEAD_DIM) & (feat < (hh + 1) * HEAD_DIM)
            own_aug = (feat >= head * AUG_ROWS) & (feat < (head + 1) * AUG_ROWS)
            qfull = jnp.concatenate([jnp.where(own, qt, jnp.zeros_like(qt)),
                                     jnp.where(own_aug, qaug, jnp.zeros_like(qaug))], axis=0)
            st = _dot_tn(k, qfull)
            if masked:
                kj = lax.broadcasted_iota(jnp.int32, (tk, tq), 0)
                qc = lax.broadcasted_iota(jnp.int32, (tk, tq), 1)
                st = jnp.where(kj <= qc, st, -jnp.inf)
            sts.append(st)
        for h in range(FLASH_HEADS):
            st = sts[h]
            m_prev = m_sc[h]
            m_new = jnp.maximum(m_prev, jnp.max(st, axis=0, keepdims=True))
            alpha = jnp.exp2(m_prev - m_new)
            p = jnp.exp2(st - m_new).astype(BF16)
            vt = jnp.concatenate([vt_ref[0, h * HEAD_DIM:(h + 1) * HEAD_DIM, :], ones_tile], axis=0)
            acc_sc[h] = alpha * acc_sc[h] + _dot(vt, p)
            m_sc[h] = m_new

    @pl.when(ki < qi)
    def _():
        step(False)

    def diag_step():
        feat = lax.broadcasted_iota(jnp.int32, (pair, 1), 0)
        kaug = kaug_ref[0]
        qaug = qaug_ref[0]
        half = tq // 2
        kj = lax.broadcasted_iota(jnp.int32, (half, half), 0)
        qc = lax.broadcasted_iota(jnp.int32, (half, half), 1)
        tri = kj <= qc
        ones_tile = (lax.broadcasted_iota(jnp.int32, (BF16_ROWS, tk), 0) == 0).astype(BF16)
        sts = []
        for h in range(FLASH_HEADS):
            hp, hh = divmod(h, 2)
            head = hg * FLASH_HEADS + h
            k = jnp.concatenate([k_ref[0, hp * pair:(hp + 1) * pair, :], kaug], axis=0)
            qt = qt_ref[0, hp * pair:(hp + 1) * pair, :]
            own = (feat >= hh * HEAD_DIM) & (feat < (hh + 1) * HEAD_DIM)
            own_aug = (feat >= head * AUG_ROWS) & (feat < (head + 1) * AUG_ROWS)
            qfull = jnp.concatenate([jnp.where(own, qt, jnp.zeros_like(qt)),
                                     jnp.where(own_aug, qaug, jnp.zeros_like(qaug))], axis=0)
            s0 = jnp.where(tri, _dot_tn(k[:, :half], qfull[:, :half]), -jnp.inf)
            s1 = _dot_tn(k, qfull[:, half:])
            s1 = jnp.concatenate([s1[:half], jnp.where(tri, s1[half:], -jnp.inf)], axis=0)
            sts.append((s0, s1))
        for h in range(FLASH_HEADS):
            s0, s1 = sts[h]
            vt = jnp.concatenate([vt_ref[0, h * HEAD_DIM:(h + 1) * HEAD_DIM, :], ones_tile], axis=0)
            m_prev = m_sc[h]
            m0 = jnp.maximum(m_prev[:, :half], jnp.max(s0, axis=0, keepdims=True))
            m1 = jnp.maximum(m_prev[:, half:], jnp.max(s1, axis=0, keepdims=True))
            p0 = jnp.exp2(s0 - m0).astype(BF16)
            p1 = jnp.exp2(s1 - m1).astype(BF16)
            a0 = jnp.exp2(m_prev[:, :half] - m0)
            a1 = jnp.exp2(m_prev[:, half:] - m1)
            acc = acc_sc[h]
            acc_sc[h] = jnp.concatenate([a0 * acc[:, :half] + _dot(vt[:, :half], p0),
                                         a1 * acc[:, half:] + _dot(vt, p1)], axis=1)
            m_sc[h] = jnp.concatenate([m0, m1], axis=1)

    @pl.when(ki == qi)
    def _():
        diag_step()
        for h in range(FLASH_HEADS):
            acc = acc_sc[h]
            o_ref[0, h * HEAD_DIM:(h + 1) * HEAD_DIM, :] = (acc[:HEAD_DIM] / acc[HEAD_DIM:HEAD_DIM + 1]).astype(o_ref.dtype)


def _flash_call(k, qt, vt, kaug, qaug, *, bsz, seq, tq):
    nq = seq // tq
    tri = [(a, b) for a in range(nq) for b in range(a + 1)]
    qi_tab = jnp.array([a for a, _ in tri], jnp.int32)
    ki_tab = jnp.array([b for _, b in tri], jnp.int32)
    width = FLASH_HEADS * HEAD_DIM
    q_spec = pl.BlockSpec((1, width, tq), lambda b, hg, s, qt_, kt_: (b, hg, qt_[s]))
    return pl.pallas_call(
        _flash_kernel,
        grid_spec=pltpu.PrefetchScalarGridSpec(
            num_scalar_prefetch=2,
            grid=(bsz, ATT_HEADS // FLASH_HEADS, len(tri)),
            in_specs=[pl.BlockSpec((1, width, tq), lambda b, hg, s, qt_, kt_: (b, hg, kt_[s])),
                      q_spec,
                      pl.BlockSpec((1, width, tq), lambda b, hg, s, qt_, kt_: (b, hg, kt_[s])),
                      pl.BlockSpec((1, LANE, tq), lambda b, hg, s, qt_, kt_: (b, 0, kt_[s])),
                      pl.BlockSpec((1, LANE, tq), lambda b, hg, s, qt_, kt_: (b, 0, qt_[s]))],
            out_specs=q_spec,
            scratch_shapes=[pltpu.VMEM((FLASH_HEADS, 1, tq), F32), pltpu.VMEM((FLASH_HEADS, VT_ROWS, tq), F32)]),
        out_shape=jax.ShapeDtypeStruct(qt.shape, BF16),
        compiler_params=_cparams(("arbitrary", "arbitrary", "arbitrary")),
        name="flash",
    )(qi_tab, ki_tab, k, qt, vt, kaug, qaug)


def _decode_kernel(pt_ref, q_ref, kn_ref, vn_ref, cn_ref, cnt_ref, hmask_ref, qmask_ref, nmask_ref, tri_ref,
                   ck_hbm, cv_hbm, clf_hbm, o_ref, kbuf, vbuf, lbuf, sem, *, layer, n_pages):
    b = pl.program_id(0)
    nb = pl.num_programs(0)
    page = lbuf.shape[3]
    n_chunks = n_pages // DEC_PAGES
    n_q = q_ref.shape[0]
    rows = n_q * ATT_HEADS

    def copies(bb, chunk, slot):
        out = []
        for j in range(DEC_PAGES):
            pg = pt_ref[bb, chunk * DEC_PAGES + j]
            cols = pl.ds(j * page, page)
            out.append(pltpu.make_async_copy(ck_hbm.at[layer, pg], kbuf.at[slot, :, cols], sem.at[slot, 0]))
            out.append(pltpu.make_async_copy(cv_hbm.at[layer, pg], vbuf.at[slot, :, cols], sem.at[slot, 1]))
            out.append(pltpu.make_async_copy(clf_hbm.at[layer, pg], lbuf.at[slot, j], sem.at[slot, 2]))
        return out

    def chunk_of(step):
        return n_chunks - 1 - step

    ahead = DEC_SLOTS - 1

    @pl.when(b == 0)
    def _():
        for step in range(ahead):
            for cp in copies(b, chunk_of(step), step % DEC_SLOTS):
                cp.start()

    q = q_ref[...].astype(F32)
    qbd = jnp.broadcast_to(q[:, None, :], (n_q, ATT_HEADS, ATT_WIDTH)).reshape(rows, ATT_WIDTH)
    qbd = jnp.where(hmask_ref[...] > 0, qbd, 0.0).astype(BF16)
    cn = cn_ref[...]
    cn_rows = jnp.broadcast_to(cn[:, None, :], (n_q, ATT_HEADS, ATT_HEADS)).reshape(rows, ATT_HEADS)
    fq = jnp.sum(jnp.where(qmask_ref[...] > 0, cn_rows, 0.0), axis=-1, keepdims=True)

    s = _dot_nt(qbd, kn_ref[...])
    fk_new = jnp.concatenate([cnt_ref[0]] * n_q, axis=0)
    s = jnp.where(nmask_ref[...] > 0, s + fq - fk_new, -jnp.inf)
    m = jnp.max(s, axis=-1, keepdims=True)
    p = jnp.exp(s - m)
    l = jnp.sum(p, axis=-1, keepdims=True)
    acc = _dot(p.astype(BF16), vn_ref[...])
    run = jnp.zeros((ATT_HEADS, 1), F32)

    for step in range(n_chunks):
        slot = step % DEC_SLOTS
        for cp in copies(b, chunk_of(step), slot):
            cp.wait()
        nxt = step + ahead
        if nxt < n_chunks:
            for cp in copies(b, chunk_of(nxt), nxt % DEC_SLOTS):
                cp.start()
        else:
            @pl.when(b + 1 < nb)
            def _(nxt=nxt):
                for cp in copies(b + 1, chunk_of(nxt - n_chunks), nxt % DEC_SLOTS):
                    cp.start()

        kc = kbuf[slot].astype(BF16)
        vc = vbuf[slot].astype(BF16)
        lf = lbuf[slot].reshape(DEC_PAGES * ATT_HEADS, page)
        hi, mid, lo = _split3(lf)
        tri = tri_ref[...]
        suf = (_dot(hi, tri) + _dot(mid, tri)) + _dot(lo, tri)
        tot = jnp.sum(lf, axis=-1, keepdims=True)
        bias = []
        for j in reversed(range(DEC_PAGES)):
            pr = slice(j * ATT_HEADS, (j + 1) * ATT_HEADS)
            bias.append(suf[pr] + run)
            run = run + tot[pr]
        bias = jnp.concatenate(bias[::-1], axis=1)
        s = _dot(qbd, kc) + fq + jnp.concatenate([bias] * n_q, axis=0)
        m_new = jnp.maximum(m, jnp.max(s, axis=-1, keepdims=True))
        alpha = jnp.exp(m - m_new)
        p = jnp.exp(s - m_new)
        l = alpha * l + jnp.sum(p, axis=-1, keepdims=True)
        acc = alpha * acc + _dot_nt(p.astype(BF16), vc)
        m = m_new

    out = jnp.where(hmask_ref[...] > 0, acc / l, 0.0)
    o_ref[...] = jnp.sum(out.reshape(n_q, ATT_HEADS, ATT_WIDTH), axis=1).astype(o_ref.dtype)


def _decode_call(page_table, q, k_new, v_new, cn, cn_t, cache_kt, cache_vt, cache_lf_t, *, layer, bsz, n_new):
    n_pages = page_table.shape[1]
    assert n_pages % (DEC_PAGES * DEC_SLOTS) == 0
    page = cache_kt.shape[3]
    rows = n_new * ATT_HEADS
    r = jnp.arange(rows)
    hmask = (r[:, None] % ATT_HEADS == jnp.arange(ATT_WIDTH)[None, :] // HEAD_DIM).astype(F32)
    qmask = (r[:, None] % ATT_HEADS == jnp.arange(ATT_HEADS)[None, :]).astype(F32)
    nmask = (r[:, None] // ATT_HEADS >= jnp.arange(n_new)[None, :]).astype(F32)
    tri = (jnp.arange(page)[:, None] > jnp.arange(page)[None, :]).astype(BF16)
    tok = lambda n: pl.BlockSpec((n_new, n), lambda b, pt: (b, 0))
    hbm = pl.BlockSpec(memory_space=pl.ANY)
    return pl.pallas_call(
        functools.partial(_decode_kernel, layer=layer, n_pages=n_pages),
        grid_spec=pltpu.PrefetchScalarGridSpec(
            num_scalar_prefetch=1,
            grid=(bsz,),
            in_specs=[tok(ATT_WIDTH), tok(ATT_WIDTH), tok(ATT_WIDTH), tok(ATT_HEADS),
                      pl.BlockSpec((1, ATT_HEADS, n_new), lambda b, pt: (b, 0, 0)),
                      _full(hmask), _full(qmask), _full(nmask), _full(tri), hbm, hbm, hbm],
            out_specs=tok(ATT_WIDTH),
            scratch_shapes=[pltpu.VMEM((DEC_SLOTS, ATT_WIDTH, DEC_PAGES * page), F32),
                            pltpu.VMEM((DEC_SLOTS, ATT_WIDTH, DEC_PAGES * page), F32),
                            pltpu.VMEM((DEC_SLOTS, DEC_PAGES, ATT_HEADS, page), F32),
                            pltpu.SemaphoreType.DMA((DEC_SLOTS, 3))]),
        out_shape=jax.ShapeDtypeStruct(q.shape, BF16),
        compiler_params=_cparams(("arbitrary",)),
        name="decode",
    )(page_table, q, k_new, v_new, cn, cn_t, hmask, qmask, nmask, tri, cache_kt, cache_vt, cache_lf_t)


def _s5_discretise(a_re, a_im, log_dt, b_re, b_im):
    dt = jnp.exp(log_dt)[:, None]
    rate = a_re * dt
    ang = a_im * dt
    mag = jnp.exp(rate)
    ab_re = mag * jnp.cos(ang)
    ab_im = mag * jnp.sin(ang)
    den = a_re * a_re + a_im * a_im
    z_re = ((ab_re - 1.0) * a_re + ab_im * a_im) / den
    z_im = (ab_im * a_re - (ab_re - 1.0) * a_im) / den
    bb_re = z_re[..., None] * b_re - z_im[..., None] * b_im
    bb_im = z_re[..., None] * b_im + z_im[..., None] * b_re
    return rate, ang, ab_re, ab_im, bb_re, bb_im


def _abar_pow(rate, ang, j):
    mag = jnp.exp(rate * j)
    return mag * jnp.cos(ang * j), mag * jnp.sin(ang * j)


def _group_diag(m):
    g, r, c = m.shape
    return jnp.einsum("grc,gh->grhc", m, jnp.eye(g, dtype=m.dtype)).reshape(g * r, g * c)


def _s5_weights(a_re, a_im, log_dt, b_re, b_im, c_re, c_im, chunks_per_seq):
    rate, ang, ab_re, ab_im, bb_re, bb_im = _s5_discretise(a_re, a_im, log_dt, b_re, b_im)
    c = S5_CHUNK
    bfull = jnp.concatenate([_group_diag(jnp.swapaxes(bb_re, 1, 2)), _group_diag(jnp.swapaxes(bb_im, 1, 2))], axis=1)
    cfull = jnp.concatenate([_group_diag(jnp.swapaxes(c_re, 1, 2)), -_group_diag(jnp.swapaxes(c_im, 1, 2))], axis=0)
    j = jnp.arange(c, dtype=F32)[:, None, None]
    p_re, p_im = _abar_pow(rate[None], ang[None], j)
    cb_re = c_re[None] * p_re[:, :, None, :] - c_im[None] * p_im[:, :, None, :]
    cb_im = c_re[None] * p_im[:, :, None, :] + c_im[None] * p_re[:, :, None, :]
    bt_re, bt_im = jnp.swapaxes(bb_re, 1, 2)[None, :, :, None, :], jnp.swapaxes(bb_im, 1, 2)[None, :, :, None, :]
    kern = jnp.sum(cb_re[:, :, None, :, :] * bt_re - cb_im[:, :, None, :, :] * bt_im, axis=-1)
    kst = jnp.concatenate([_group_diag(kern[c - 1 - s]) for s in range(c)], axis=0)
    n_steps = max(1, (chunks_per_seq - 1).bit_length())
    d = (c * 2.0 ** jnp.arange(n_steps, dtype=F32))[:, None, None]
    s_re, s_im = _abar_pow(rate[None], ang[None], d)
    flat = lambda a: a.reshape(a.shape[0], 1, N_STATE)
    return dict(a_re=ab_re.reshape(1, N_STATE), a_im=ab_im.reshape(1, N_STATE), b=bfull.astype(BF16),
                c=cfull.astype(BF16), kst=kst.astype(BF16), s_re=flat(s_re), s_im=flat(s_im))


def _s5_seq_kernel(ulo_ref, uhi_ref, d_ref, b_ref, c_ref, kst_ref, are_ref, aim_ref, sre_ref, sim_ref,
                   ylo_ref, yhi_ref, hre_ref, him_ref, u2_sc):
    w = SSM_WIDTH
    n = ulo_ref.shape[0] // S5_CHUNK
    a_re = are_ref[...]
    a_im = aim_ref[...]
    token = lambda ref, s: ref[pl.ds(s, n, stride=S5_CHUNK), :]
    h_re = h_im = None
    for s in range(S5_CHUNK):
        ub = jnp.concatenate([token(ulo_ref, s), token(uhi_ref, s)], axis=1).astype(BF16)
        u2_sc[:, s * w:(s + 1) * w] = ub
        x = _dot(ub, b_ref[...])
        x_re, x_im = x[:, :N_STATE], x[:, N_STATE:]
        if s == 0:
            h_re, h_im = x_re, x_im
        else:
            h_re, h_im = a_re * h_re - a_im * h_im + x_re, a_re * h_im + a_im * h_re + x_im
    pos = lax.broadcasted_iota(jnp.int32, (n, 1), 0)
    for k in range(sre_ref.shape[0]):
        d = 1 << k
        keep = pos >= d
        p_re = jnp.where(keep, pltpu.roll(h_re, d, 0), 0.0)
        p_im = jnp.where(keep, pltpu.roll(h_im, d, 0), 0.0)
        s_re = sre_ref[k]
        s_im = sim_ref[k]
        h_re, h_im = h_re + (s_re * p_re - s_im * p_im), h_im + (s_re * p_im + s_im * p_re)
    hre_ref[0] = h_re[n - 1:n]
    him_ref[0] = h_im[n - 1:n]
    keep = pos >= 1
    g_re = jnp.where(keep, pltpu.roll(h_re, 1, 0), 0.0)
    g_im = jnp.where(keep, pltpu.roll(h_im, 1, 0), 0.0)
    for t in range(S5_CHUNK):
        g_re, g_im = a_re * g_re - a_im * g_im, a_re * g_im + a_im * g_re
        g = jnp.concatenate([g_re, g_im], axis=1).astype(BF16)
        y = _dot(u2_sc[:, 0:(t + 1) * w], kst_ref[(S5_CHUNK - 1 - t) * w:, :]) + _dot(g, c_ref[...])
        ylo_ref[pl.ds(t, n, stride=S5_CHUNK), :] = y[:, :LANE] + d_ref[:, :LANE] * token(ulo_ref, t)
        yhi_ref[pl.ds(t, n, stride=S5_CHUNK), :] = y[:, LANE:] + d_ref[:, LANE:] * token(uhi_ref, t)


def _s5_seq_call(u_lo, u_hi, dskip, wts, *, bsz, seq):
    cps = seq // S5_CHUNK
    names = ("b", "c", "kst", "a_re", "a_im", "s_re", "s_im")
    half = pl.BlockSpec((seq, LANE), lambda i: (i, 0))
    state_spec = pl.BlockSpec((1, 1, N_STATE), lambda i: (i, 0, 0))
    y_lo, y_hi, h_re, h_im = pl.pallas_call(
        _s5_seq_kernel,
        grid=(bsz,),
        in_specs=[half, half, _full(dskip)] + [_full(wts[k]) for k in names],
        out_specs=[half, half, state_spec, state_spec],
        out_shape=[jax.ShapeDtypeStruct(u_lo.shape, F32), jax.ShapeDtypeStruct(u_lo.shape, F32),
                   jax.ShapeDtypeStruct((bsz, 1, N_STATE), F32), jax.ShapeDtypeStruct((bsz, 1, N_STATE), F32)],
        scratch_shapes=[pltpu.VMEM((cps, S5_CHUNK * SSM_WIDTH), BF16)],
        compiler_params=_cparams(("arbitrary",)),
        name="s5_seq",
    )(*_raw([u_lo, u_hi, dskip] + [wts[k] for k in names]))
    state = lambda h: h.reshape(bsz, SSM_GROUPS, SSM_STATE)
    return y_lo, y_hi, state(h_re), state(h_im)


def _s5_step_kernel(u_ref, d_ref, h0re_ref, h0im_ref, are_ref, aim_ref, b_ref, c_ref, y_ref, hre_ref, him_ref, *,
                    n_new):
    h_re = h0re_ref[...]
    h_im = h0im_ref[...]
    a_re = are_ref[...]
    a_im = aim_ref[...]
    for t in range(n_new):
        ut = u_ref[t]
        x = _dot(ut.astype(BF16), b_ref[...])
        h_re, h_im = (a_re * h_re - a_im * h_im + x[:, :N_STATE], a_re * h_im + a_im * h_re + x[:, N_STATE:])
        y_ref[t] = _dot(jnp.concatenate([h_re, h_im], axis=1).astype(BF16), c_ref[...]) + d_ref[...] * ut
    hre_ref[...] = h_re
    him_ref[...] = h_im


def _s5_step_call(u, dskip, h0_re, h0_im, wts, *, bsz, n_new):
    u_t = jnp.swapaxes(u.reshape(bsz, n_new, SSM_WIDTH), 0, 1)
    ins = [u_t, dskip, h0_re, h0_im, wts["a_re"], wts["a_im"], wts["b"], wts["c"]]
    outs = [jax.ShapeDtypeStruct(u_t.shape, F32), jax.ShapeDtypeStruct((bsz, N_STATE), F32),
            jax.ShapeDtypeStruct((bsz, N_STATE), F32)]
    y, h_re, h_im = pl.pallas_call(
        functools.partial(_s5_step_kernel, n_new=n_new),
        grid=(1,),
        in_specs=[_full(a) for a in ins],
        out_specs=[_full(o) for o in outs],
        out_shape=outs,
        compiler_params=_cparams(("arbitrary",)),
        name="s5_step",
    )(*_raw(ins))
    y = jnp.swapaxes(y, 0, 1).reshape(bsz * n_new, SSM_WIDTH)
    return (y[:, :LANE], y[:, LANE:], h_re.reshape(bsz, SSM_GROUPS, SSM_STATE),
            h_im.reshape(bsz, SSM_GROUPS, SSM_STATE))


def _merge_kernel(x_ref, sh_ref, sc_ref, gt_ref, g1_ref, wg_ref, yslo_ref, yshi_ref, wglu_ref,
                  gated_ref, wgm_ref, att_ref, wat_ref, wo_ref, o_ref, *, att_feature_major):
    x = x_ref[...]
    h = (_rms(x) * g1_ref[...] * (1.0 + sc_ref[0]) + sh_ref[0]).astype(BF16)
    y_s = jnp.concatenate([yslo_ref[...], yshi_ref[...]], axis=1)
    glu = _dot(_gelu(y_s).astype(BF16), wglu_ref[...])
    d = D_MODEL
    y_att = _dot_tn(att_ref[0], wat_ref[...]) if att_feature_major else _dot(att_ref[...], wat_ref[...])
    merged = jax.nn.sigmoid(_dot(h, wg_ref[:, 0:d])) * (glu[:, :d] * jax.nn.sigmoid(glu[:, d:]))
    merged = merged + jax.nn.sigmoid(_dot(h, wg_ref[:, d:2 * d])) * _dot(gated_ref[...], wgm_ref[...])
    merged = merged + jax.nn.sigmoid(_dot(h, wg_ref[:, 2 * d:3 * d])) * y_att
    o_ref[...] = x + gt_ref[0] * _dot(merged.astype(BF16), wo_ref[...])


def _merge_call(x, ada, g1, wg, ys_lo, ys_hi, wglu, gated, wgm, att, wat, wo, *, tm, tiles_per_batch):
    t = x.shape[0]
    row = lambda n: pl.BlockSpec((tm, n), lambda i: (i, 0))
    fm = att.ndim == 3
    att_spec = (pl.BlockSpec((1, ATT_WIDTH, tm), lambda i: (i // tiles_per_batch, 0, i % tiles_per_batch))
                if fm else row(ATT_WIDTH))
    return pl.pallas_call(
        functools.partial(_merge_kernel, att_feature_major=fm),
        grid=(t // tm,),
        in_specs=[row(D_MODEL), _mod_spec(ada, tiles_per_batch, 0), _mod_spec(ada, tiles_per_batch, 1),
                  _mod_spec(ada, tiles_per_batch, 2), _full(g1), _full(wg), row(LANE), row(LANE),
                  _full(wglu), row(GMLP_WIDTH), _full(wgm), att_spec, _full(wat), _full(wo)],
        out_specs=row(D_MODEL),
        out_shape=jax.ShapeDtypeStruct(x.shape, F32),
        compiler_params=_cparams(("arbitrary",)),
        name="merge_out",
    )(*_raw([x, ada, ada, ada, g1, wg, ys_lo, ys_hi, wglu, gated, wgm, att, wat, wo]))


def _mlp_kernel(x_ref, sh_ref, sc_ref, gt_ref, g2_ref, wup_ref, wdn_ref, gfin_ref, o_ref, *, final):
    x = x_ref[...]
    h = (_rms(x) * g2_ref[...] * (1.0 + sc_ref[0]) + sh_ref[0]).astype(BF16)
    blk = D_FF // FF_SPLIT
    acc = None
    for c in range(FF_SPLIT):
        a = jnp.maximum(_dot(h, wup_ref[:, c * blk:(c + 1) * blk]), 0.0)
        part = _dot((a * a).astype(BF16), wdn_ref[c * blk:(c + 1) * blk, :])
        acc = part if acc is None else acc + part
    out = x + gt_ref[0] * acc
    o_ref[...] = _rms(out) * gfin_ref[...] if final else out


def _mlp_call(x, ada, g2, wup, wdn, gfin, *, tm, tiles_per_batch, final):
    t = x.shape[0]
    row = lambda n: pl.BlockSpec((tm, n), lambda i: (i, 0))
    return pl.pallas_call(
        functools.partial(_mlp_kernel, final=final),
        grid=(t // tm,),
        in_specs=[row(D_MODEL), _mod_spec(ada, tiles_per_batch, 3), _mod_spec(ada, tiles_per_batch, 4),
                  _mod_spec(ada, tiles_per_batch, 5), _full(g2), _full(wup), _full(wdn), _full(gfin)],
        out_specs=row(D_MODEL),
        out_shape=jax.ShapeDtypeStruct(x.shape, F32),
        compiler_params=_cparams(("arbitrary",)),
        name="mlp",
    )(*_raw([x, ada, ada, ada, g2, wup, wdn, gfin]))


def _mix_weights(w_s, b_s, seq, rows):
    cl = min(seq, CHUNK)
    w = jnp.tril(w_s[:, :, :cl, :cl])
    r = jnp.arange(rows)
    pos = r % cl
    tiled = jnp.take(jnp.take(w, pos, axis=2), pos, axis=3)
    wmix = jnp.where(r[:, None] // cl == r[None, :] // cl, tiled, 0.0).astype(BF16)
    bias = jnp.take(jnp.swapaxes(b_s[:, :, :cl], 1, 2), pos, axis=1)
    bmix = jnp.repeat(bias, GMLP_GROUP_CH, axis=2)
    return wmix, bmix


def _aug_constants():
    hd = jnp.arange(ATT_HEADS)
    pk = jnp.zeros((LANE, 4 * ATT_HEADS), F32)
    pq = jnp.zeros((LANE, 4 * ATT_HEADS), F32)
    ones_k = jnp.zeros((LANE, 1), F32)
    ones_q = jnp.zeros((LANE, 1), F32)
    for r in range(3):
        pk = pk.at[hd * AUG_ROWS + 3 + r, r * ATT_HEADS + hd].set(-1.0)
        pq = pq.at[hd * AUG_ROWS + r, r * ATT_HEADS + hd].set(1.0)
        ones_k = ones_k.at[hd * AUG_ROWS + r, 0].set(1.0)
        ones_q = ones_q.at[hd * AUG_ROWS + 3 + r, 0].set(1.0)
    return pk.astype(BF16), ones_k, pq.astype(BF16), ones_q


def _cumsum_tri(rows, seq):
    r = jnp.arange(rows)
    return ((r[:, None] >= r[None, :]) & (r[:, None] // seq == r[None, :] // seq)).astype(BF16)


def kernel(x_prompt, x_sample, c_prompt, c_sample, cache_k, cache_v, cache_logf, state_ssm_re, state_ssm_im, page_table, w_ada, b_ada, g_norm1, w_in, b_f, ssm_a_re, ssm_a_im, ssm_log_dt, ssm_b_re, ssm_b_im, ssm_c_re, ssm_c_im, ssm_d, w_glu, g_gv, w_s, b_s, w_gmlp_out, w_att_out, w_o, g_norm2, w_up, w_down, g_final):
    depth = w_in.shape[0]
    bp, lp, d = x_prompt.shape
    bs, ls, _ = x_sample.shape
    tp, ts = bp * lp, bs * ls
    tm_p = TM_PROMPT
    tiles_pb = lp // tm_p

    ada = _ada_call(jnp.concatenate([c_prompt, c_sample], axis=0), w_ada, b_ada)
    ada_p = ada[:, :bp].reshape(depth, bp, 1, 6 * d)
    ada_s = jnp.repeat(ada[:, bp:], ls, axis=1).reshape(depth, 1, ts, 6 * d)
    n_pool, page = cache_k.shape[1], cache_k.shape[2]
    feature_major = lambda c: jnp.transpose(c, (0, 1, 3, 4, 2)).reshape(depth, n_pool, ATT_WIDTH, page)
    cache_kt = feature_major(cache_k)
    cache_vt = feature_major(cache_v)
    cache_lf_t = jnp.swapaxes(cache_logf, 2, 3)

    pad_f = lambda w: jnp.pad(w, ((0, 0), (0, 0), (0, LANE - ATT_HEADS)))
    w_qkv, w_f, w_sgg = w_in[:, :, Q_OFF:F_OFF], w_in[:, :, F_OFF:S_OFF], w_in[:, :, S_OFF:G_OFF]
    stacked = dict(
        w_row_p=w_sgg.astype(BF16),
        w_t_p=jnp.pad(jnp.swapaxes(jnp.concatenate([w_qkv, w_f], axis=2), 1, 2),
                      ((0, 0), (0, BF16_ROWS - ATT_HEADS), (0, 0))).astype(BF16),
        w_row_s=jnp.concatenate([w_qkv, w_sgg, pad_f(w_f)], axis=2).astype(BF16),
        w_gate=w_in[:, :, G_OFF:].astype(BF16),
        bf_row=jnp.pad(b_f, ((0, 0), (0, LANE - ATT_HEADS))).reshape(depth, 1, LANE),
        bf_col=b_f.reshape(depth, ATT_HEADS, 1),
        g1=g_norm1.reshape(depth, 1, d), g2=g_norm2.reshape(depth, 1, d),
        ggv=g_gv.reshape(depth, 1, GMLP_WIDTH), dskip=ssm_d.reshape(depth, 1, SSM_WIDTH),
        wglu=w_glu.astype(BF16), wgm=w_gmlp_out.astype(BF16), wat=w_att_out.astype(BF16), wo=w_o.astype(BF16),
        wup=w_up.astype(BF16), wdn=w_down.astype(BF16),
        h0_re=state_ssm_re.reshape(depth, bs, N_STATE), h0_im=state_ssm_im.reshape(depth, bs, N_STATE),
        ada_p=ada_p, ada_s=ada_s)
    stacked["wmix_p"], stacked["bmix_p"] = _mix_weights(w_s, b_s, lp, CHUNK)
    stacked["wmix_s"], stacked["bmix_s"] = _mix_weights(w_s, b_s, ls, ts)
    s5_stacked = jax.vmap(functools.partial(_s5_weights, chunks_per_seq=lp // S5_CHUNK))(
        ssm_a_re, ssm_a_im, ssm_log_dt, ssm_b_re, ssm_b_im, ssm_c_re, ssm_c_im)
    tri_p = _cumsum_tri(tm_p, tm_p)
    tri_s = _cumsum_tri(ts, ls)
    gfin = g_final.reshape(1, d)
    aug = _aug_constants()

    xp = x_prompt.reshape(tp, d)
    xs = x_sample.reshape(ts, d)
    outs = {k: [] for k in ("kp", "vp", "lfp", "hrp", "hip", "gvp", "ks", "vs", "lfs", "hrs", "his", "gvs")}
    token_major = lambda a: jnp.transpose(a.reshape(bp, ATT_HEADS, HEAD_DIM, lp), (0, 3, 1, 2))
    for l in range(depth):
        w = {k: _Layer(v, l) for k, v in stacked.items()}
        s5w = {k: _Layer(v, l) for k, v in s5_stacked.items()}

        qt, ktf, ktb, vtf, vtb, lft, kaug, qaug, u_lo, u_hi, gated, gv = _in_proj_p_call(
            xp, w["ada_p"], w["g1"], w["w_row_p"], w["w_t_p"], w["bf_col"], w["ggv"], w["wmix_p"], w["bmix_p"],
            tri_p.T, aug, bsz=bp, seq=lp, tm=tm_p)
        att_t = _flash_call(ktb, qt, vtb, kaug, qaug, bsz=bp, seq=lp, tq=tm_p)
        ys_lo, ys_hi, hr, hi = _s5_seq_call(u_lo, u_hi, w["dskip"], s5w, bsz=bp, seq=lp)
        xp = _merge_call(xp, w["ada_p"], w["g1"], w["w_gate"], ys_lo, ys_hi, w["wglu"], gated, w["wgm"], att_t,
                         w["wat"], w["wo"], tm=tm_p, tiles_per_batch=tiles_pb)
        xp = _mlp_call(xp, w["ada_p"], w["g2"], w["wup"], w["wdn"], gfin, tm=tm_p, tiles_per_batch=tiles_pb,
                       final=l == depth - 1)
        outs["kp"].append(token_major(ktf))
        outs["vp"].append(token_major(vtf))
        outs["lfp"].append(jnp.swapaxes(lft, 1, 2))
        outs["hrp"].append(hr)
        outs["hip"].append(hi)
        outs["gvp"].append(gv)

        q, kf, kb, vf, vb, lf, cum, u, gated, gv = _in_proj_s_call(
            xs, w["ada_s"], w["g1"], w["w_row_s"], w["bf_row"], w["ggv"], w["wmix_s"], w["bmix_s"], tri_s)
        cum_t = jnp.swapaxes(cum.reshape(bs, ls, ATT_HEADS), 1, 2)
        att = _decode_call(page_table, q, kb, vb, cum, cum_t, cache_kt, cache_vt, cache_lf_t,
                           layer=l, bsz=bs, n_new=ls)
        ys_lo, ys_hi, hr, hi = _s5_step_call(u, w["dskip"], w["h0_re"], w["h0_im"], s5w, bsz=bs, n_new=ls)
        xs = _merge_call(xs, w["ada_s"], w["g1"], w["w_gate"], ys_lo, ys_hi, w["wglu"], gated, w["wgm"], att,
                         w["wat"], w["wo"], tm=ts, tiles_per_batch=1)
        xs = _mlp_call(xs, w["ada_s"], w["g2"], w["wup"], w["wdn"], gfin, tm=ts, tiles_per_batch=1,
                       final=l == depth - 1)
        outs["ks"].append(kf.reshape(bs, ls, ATT_HEADS, HEAD_DIM))
        outs["vs"].append(vf.reshape(bs, ls, ATT_HEADS, HEAD_DIM))
        outs["lfs"].append(lf.reshape(bs, ls, ATT_HEADS))
        outs["hrs"].append(hr)
        outs["his"].append(hi)
        outs["gvs"].append(gv.reshape(bs, ls, GMLP_WIDTH))

    st = lambda k: jnp.stack(outs[k])
    return (xp.reshape(bp, lp, d), xs.reshape(bs, ls, d), st("kp"), st("vp"), st("lfp"), st("hrp"), st("hip"), st("gvp"),
            st("ks"), st("vs"), st("lfs"), st("hrs"), st("his"), st("gvs"))
```
